```python
import math
import jax
import jax.numpy as jnp
from jax import lax
import numpy as np

D_MODEL = 2048
BATCH = 4
SEQ = 4096
DEPTH = 4

CTX_LEN = 256
GRID_W = 64
HEAD_DIM = 128
N_GROUPS = 4
GROUP_WIDTH = D_MODEL // N_GROUPS
GROUP_HEADS = GROUP_WIDTH // HEAD_DIM
D_MIX = N_GROUPS * GROUP_WIDTH
GDN_CONV = 5
GDN_CHUNK = 64
GLA_DK = HEAD_DIM // 2
GLA_RANK = 16
GLA_TAU = 16.0
DIAG_CHUNK = 16
NA_ROWS = 8
NA_COLS = 16
ROPE_BASE = 10000.0
N_EXPERTS = 16
EC_CAPACITY = 2
D_EXPERT = D_MODEL // 2
DEEPNORM_ALPHA = (2 * DEPTH) ** 0.25
DEEPNORM_BETA = (8 * DEPTH) ** -0.25
LN_EPS = 1e-5
NORM_EPS = 1e-6

SPLIT_SIZES = (
    GROUP_WIDTH, GROUP_WIDTH, GROUP_WIDTH, GROUP_WIDTH, 2 * GROUP_HEADS, 2 * GROUP_HEADS,
    GROUP_HEADS * GLA_DK, GROUP_HEADS * GLA_DK, GROUP_WIDTH, GROUP_WIDTH, 2 * GLA_RANK,
    GROUP_WIDTH, GROUP_WIDTH, GROUP_WIDTH, 2 * GROUP_WIDTH,
    GROUP_WIDTH, GROUP_WIDTH, GROUP_WIDTH,
)
N_IN = sum(SPLIT_SIZES)
SPLIT_POINTS = [sum(SPLIT_SIZES[:i + 1]) for i in range(len(SPLIT_SIZES) - 1)]

kernel_name = 'hybrid_flow_backbone'

F32 = jnp.float32


def layer_norm(x, g, b):
    xf = x.astype(F32)
    mu = jnp.mean(xf, axis=-1, keepdims=True)
    var = jnp.mean(jnp.square(xf - mu), axis=-1, keepdims=True)
    y = (xf - mu) * lax.rsqrt(var + LN_EPS)
    return (y * g.astype(F32) + b.astype(F32)).astype(x.dtype)


def modulate(x, shift, scale):
    xf = x.astype(F32)
    mu = jnp.mean(xf, axis=-1, keepdims=True)
    var = jnp.mean(jnp.square(xf - mu), axis=-1, keepdims=True)
    y = (xf - mu) * lax.rsqrt(var + LN_EPS)
    return (y * (1.0 + scale.astype(F32)) + shift.astype(F32)).astype(x.dtype)


def rms_norm(x, g):
    xf = x.astype(F32)
    return xf * lax.rsqrt(jnp.mean(jnp.square(xf), axis=-1, keepdims=True) + NORM_EPS) * g.astype(F32)


def l2_normalize(x):
    return x * lax.rsqrt(jnp.sum(jnp.square(x), axis=-1, keepdims=True) + NORM_EPS)


def to_heads(t, n):
    bn, tt, w = t.shape
    return t.reshape(bn, tt, n, w // n).transpose(0, 2, 1, 3)


def from_heads(t):
    bn, n, tt, d = t.shape
    return t.transpose(0, 2, 1, 3).reshape(bn, tt, n * d)


def short_conv(x, w):
    k = w.shape[0]
    return lax.conv_general_dilated(x, w[:, None, :].astype(x.dtype), (1,), [(k // 2, k // 2)],
                                    dimension_numbers=('NWC', 'WIO', 'NWC'),
                                    feature_group_count=x.shape[-1])


def axial_rope(x):
    length, dk = x.shape[2], x.shape[3]
    half = dk // 2
    nf = half // 2
    t = jnp.arange(length)
    inv = ROPE_BASE ** (-jnp.arange(nf, dtype=F32) / nf)

    def rot(xh, pos):
        ang = pos.astype(F32)[:, None] * inv
        cos, sin = jnp.cos(ang).astype(xh.dtype), jnp.sin(ang).astype(xh.dtype)
        x1, x2 = xh[..., :nf], xh[..., nf:]
        return jnp.concatenate([x1 * cos - x2 * sin, x1 * sin + x2 * cos], axis=-1)

    return jnp.concatenate([rot(x[..., :half], t // GRID_W), rot(x[..., half:], t % GRID_W)], axis=-1)


def reverse_segments(z, tc):
    return jnp.concatenate([jnp.flip(z[:, :, :tc], 2), jnp.flip(z[:, :, tc:], 2)], axis=2)


def bidirectional(run, tc, shared, fwd, bwd):
    o_f = run(*shared, *fwd)
    o_b = run(*(reverse_segments(t, tc) for t in shared), *(reverse_segments(t, tc) for t in bwd))
    return o_f + reverse_segments(o_b, tc)


def gated_delta_chunked(q, k, v, log_a, beta):
    bn, h, tt, dk = q.shape
    dv = v.shape[-1]
    c = GDN_CHUNK
    n = tt // c
    q = q.astype(F32).reshape(bn, h, n, c, dk)
    k = k.astype(F32).reshape(bn, h, n, c, dk)
    v = v.astype(F32).reshape(bn, h, n, c, dv)
    beta = beta.astype(F32).reshape(bn, h, n, c)
    g = jnp.cumsum(log_a.astype(F32).reshape(bn, h, n, c), axis=-1)
    incl = jnp.tril(jnp.ones((c, c), bool))
    strict = jnp.tril(jnp.ones((c, c), bool), -1)
    diff = g[..., :, None] - g[..., None, :]
    dec_strict = jnp.exp(jnp.where(strict, diff, -jnp.inf))
    dec_incl = jnp.exp(jnp.where(incl, diff, -jnp.inf))
    a_mat = beta[..., :, None] * jnp.einsum('bhnid,bhnjd->bhnij', k, k) * dec_strict
    rhs = jnp.concatenate([beta[..., None] * v, (beta * jnp.exp(g))[..., None] * k], axis=-1)
    sol = lax.linalg.triangular_solve(a_mat + jnp.eye(c, dtype=F32), rhs, left_side=True,
                                      lower=True, unit_diagonal=True)
    u, w = sol[..., :dv], sol[..., dv:]
    p = jnp.einsum('bhnid,bhnjd->bhnij', q, k) * dec_incl
    qg = q * jnp.exp(g)[..., None]
    kd = k * jnp.exp(g[..., -1:] - g)[..., None]
    g_end = jnp.exp(g[..., -1])

    def step(s, xs):
        qg_n, w_n, u_n, p_n, kd_n, ge_n = xs
        v_new = u_n - jnp.einsum('bhcd,bhde->bhce', w_n, s)
        o_n = jnp.einsum('bhcd,bhde->bhce', qg_n, s) + jnp.einsum('bhij,bhje->bhie', p_n, v_new)
        s = ge_n[..., None, None] * s + jnp.einsum('bhcd,bhce->bhde', kd_n, v_new)
        return s, o_n

    xs = tuple(jnp.moveaxis(t, 2, 0) for t in (qg, w, u, p, kd, g_end))
    _, o = lax.scan(step, jnp.zeros((bn, h, dk, dv), F32), xs)
    return jnp.moveaxis(o, 0, 2).reshape(bn, h, tt, dv)


def diag_gated_chunked(q, k, v, log_a):
    bn, h, tt, dk = q.shape
    dv = v.shape[-1]
    c = DIAG_CHUNK
    n = tt // c
    q, k, log_a = (t.astype(F32).reshape(bn, h, n, c, dk) for t in (q, k, log_a))
    v = v.astype(F32).reshape(bn, h, n, c, dv)
    b = jnp.cumsum(log_a, axis=3)
    incl = jnp.tril(jnp.ones((c, c), bool))
    diff = b[:, :, :, :, None, :] - b[:, :, :, None, :, :]
    decay = jnp.exp(jnp.where(incl[:, :, None], diff, -jnp.inf))
    p = jnp.sum(q[:, :, :, :, None, :] * k[:, :, :, None, :, :] * decay, axis=-1)
    qg = q * jnp.exp(b)
    kd = k * jnp.exp(b[:, :, :, -1:, :] - b)
    a_end = jnp.exp(b[:, :, :, -1, :])

    def step(s, xs):
        qg_n, kd_n, v_n, p_n, a_n = xs
        o_n = jnp.einsum('bhcd,bhde->bhce', qg_n, s) + jnp.einsum('bhij,bhje->bhie', p_n, v_n)
        s = a_n[..., None] * s + jnp.einsum('bhcd,bhce->bhde', kd_n, v_n)
        return s, o_n

    xs = tuple(jnp.moveaxis(t, 2, 0) for t in (qg, kd, v, p, a_end))
    _, o = lax.scan(step, jnp.zeros((bn, h, dk, dv), F32), xs)
    return jnp.moveaxis(o, 0, 2).reshape(bn, h, tt, dv)


def hgrn_run(q, i_val, k_f, log_f):
    return diag_gated_chunked(q, k_f, i_val, log_f)


def neighbourhood_attention(q_lat, k_lat, v_lat, q_ctx, k_ctx, v_ctx, rpb):
    bn, h, length, d = q_lat.shape
    rows = length // GRID_W
    kr = min(NA_ROWS, rows)
    kc = NA_COLS
    scale = d ** -0.5
    r = np.arange(rows)
    cq = np.arange(GRID_W)
    row_idx = np.clip(r - kr // 2, 0, rows - kr)[:, None] + np.arange(kr)[None, :]
    col_start = np.clip(cq - kc // 2, 0, GRID_W - kc)
    col_in = (cq[None, :] >= col_start[:, None]) & (cq[None, :] < col_start[:, None] + kc)
    dr = row_idx - r[:, None] + NA_ROWS - 1
    dc = np.clip(cq[None, :] - cq[:, None], -(kc - 1), kc - 1) + NA_COLS - 1
    bias = rpb.astype(F32)[:, dr[:, None, :, None], dc[None, :, None, :]]
    qg = q_lat.reshape(bn, h, rows, GRID_W, d)
    k_blk = k_lat.reshape(bn, h, rows, GRID_W, d)[:, :, row_idx]
    v_blk = v_lat.reshape(bn, h, rows, GRID_W, d)[:, :, row_idx]
    s_loc = jnp.einsum('bhrqd,bhrjkd->bhrqjk', qg, k_blk).astype(F32) * scale + bias[None]
    s_loc = jnp.where(col_in[:, None, :], s_loc, -jnp.inf)
    s_ctx = jnp.einsum('bhrqd,bhcd->bhrqc', qg, k_ctx).astype(F32) * scale
    s = jnp.concatenate([s_loc.reshape(bn, h, rows, GRID_W, kr * GRID_W), s_ctx], axis=-1)
    pr = jax.nn.softmax(s, axis=-1).astype(v_lat.dtype)
    p_loc = pr[..., :kr * GRID_W].reshape(bn, h, rows, GRID_W, kr, GRID_W)
    p_ctx = pr[..., kr * GRID_W:]
    o_lat = (jnp.einsum('bhrqjk,bhrjkd->bhrqd', p_loc, v_blk)
             + jnp.einsum('bhrqc,bhcd->bhrqd', p_ctx, v_ctx)).reshape(bn, h, length, d)
    s_cc = jnp.einsum('bhqd,bhkd->bhqk', q_ctx, k_ctx).astype(F32) * scale
    o_ctx = jnp.einsum('bhqk,bhkd->bhqd', jax.nn.softmax(s_cc, axis=-1).astype(v_ctx.dtype), v_ctx)
    return o_ctx, o_lat


def hybrid_mixer(u_ctx, u_lat, w_in, conv_w, a_log, dt_bias, gdn_g, gla_wg, gla_bg, gla_g, lb, hgrn_g, rpb):
    tc = u_ctx.shape[1]
    u = jnp.concatenate([u_ctx, u_lat], axis=1)
    bn, tt, _ = u.shape
    h = GROUP_HEADS
    (a_q, a_k, a_v, a_z, a_beta, a_dec, b_q, b_k, b_v, b_r, b_code,
     c_q, c_i, c_g, c_f, d_q, d_k, d_v) = jnp.split(jnp.einsum('btd,de->bte', u, w_in), SPLIT_POINTS, axis=-1)

    qkv = jnp.concatenate([a_q, a_k, a_v], axis=-1)
    qkv = jax.nn.silu(jnp.concatenate([short_conv(qkv[:, :tc], conv_w), short_conv(qkv[:, tc:], conv_w)], axis=1))
    q, k, v = (to_heads(t, h).astype(F32) for t in jnp.split(qkv, 3, axis=-1))
    q = l2_normalize(q) * HEAD_DIM ** -0.5
    k = l2_normalize(k)
    beta = jnp.transpose(jax.nn.sigmoid(a_beta.astype(F32)).reshape(bn, tt, 2, h), (2, 0, 3, 1))
    log_a = -jnp.exp(a_log.astype(F32)) * jax.nn.softplus(a_dec.astype(F32).reshape(bn, tt, 2, h) + dt_bias.astype(F32))
    log_a = jnp.transpose(log_a, (2, 0, 3, 1))
    o = bidirectional(gated_delta_chunked, tc, (q, k, v), (log_a[0], beta[0]), (log_a[1], beta[1]))
    o_gdn = from_heads(rms_norm(o, gdn_g)) * jax.nn.silu(a_z.astype(F32))

    q = to_heads(b_q, h).astype(F32) * GLA_DK ** -0.5
    k = to_heads(b_k, h).astype(F32)
    q = jnp.concatenate([q[:, :, :tc], axial_rope(q[:, :, tc:])], axis=2)
    k = jnp.concatenate([k[:, :, :tc], axial_rope(k[:, :, tc:])], axis=2)
    v = to_heads(b_v, h).astype(F32)
    gate = jnp.einsum('btzr,zrk->zbtk', b_code.astype(F32).reshape(bn, tt, 2, GLA_RANK), gla_wg.astype(F32))
    gate = gate + gla_bg.astype(F32)[:, None, None, :]
    log_g = (jax.nn.log_sigmoid(gate) / GLA_TAU).reshape(2, bn, tt, h, GLA_DK).transpose(0, 1, 3, 2, 4)
    o = bidirectional(diag_gated_chunked, tc, (q, k, v), (log_g[0],), (log_g[1],))
    o_gla = from_heads(rms_norm(o, gla_g)) * jax.nn.silu(b_r.astype(F32))

    q = to_heads(c_q, h).astype(F32)
    i_val = to_heads(c_i, h).astype(F32)
    lbf = lb.astype(F32)
    f = lbf + (1.0 - lbf) * jax.nn.sigmoid(c_f.astype(F32).reshape(bn, tt, 2, GROUP_WIDTH))
    f = f.reshape(bn, tt, 2, h, HEAD_DIM).transpose(2, 0, 3, 1, 4)
    log_f = jnp.log(f)
    k_f = 1.0 - f
    o = bidirectional(hgrn_run, tc, (q, i_val), (k_f[0], log_f[0]), (k_f[1], log_f[1]))
    o_hgrn = from_heads(rms_norm(o, hgrn_g)) * jax.nn.sigmoid(c_g.astype(F32))

    q, k, v = (to_heads(t, h) for t in (d_q, d_k, d_v))
    o_c, o_l = neighbourhood_attention(q[:, :, tc:], k[:, :, tc:], v[:, :, tc:],
                                       q[:, :, :tc], k[:, :, :tc], v[:, :, :tc], rpb)
    o_na = from_heads(jnp.concatenate([o_c, o_l], axis=2)).astype(F32)

    mixed = jnp.concatenate([o_gdn, o_gla, o_hgrn, o_na], axis=-1).astype(u.dtype)
    return mixed[:, :tc], mixed[:, tc:]


def expert_choice_ffn(u, w_router, w1, w3, w2):
    bn, tt, _ = u.shape
    cap = EC_CAPACITY * tt // N_EXPERTS
    aff = jax.nn.softmax(jnp.einsum('btd,de->bte', u, w_router).astype(F32), axis=-1)
    gate, idx = lax.top_k(jnp.swapaxes(aff, 1, 2), cap)
    bidx = jnp.arange(bn)[:, None, None]
    xs = u[bidx, idx]
    hid = jax.nn.silu(jnp.einsum('becd,edf->becf', xs, w1)) * jnp.einsum('becd,edf->becf', xs, w3)
    y = jnp.einsum('becf,efd->becd', hid, w2) * gate[..., None].astype(u.dtype)
    return jnp.zeros_like(u).at[bidx, idx].add(y)


def setup_inputs(seed: int = 0) -> dict:
    key = jax.random.key(seed)
    ks = jax.random.split(key, 24)
    d = D_MODEL

    def nrm(k, shape, scale):
        return jax.random.normal(k, shape, F32) * scale

    dt = jnp.exp(jax.random.uniform(ks[12], (DEPTH, 2, GROUP_HEADS), F32, math.log(1e-3), math.log(1e-1)))
    return {
        'x': nrm(ks[0], (BATCH, SEQ, d), 1.0),
        'c': nrm(ks[1], (BATCH, d), 1.0),
        'ctx': nrm(ks[2], (BATCH, CTX_LEN, d), 1.0),
        'c_ctx': nrm(ks[3], (d,), 1.0),
        'w_mod': nrm(ks[4], (DEPTH, d, 6 * d), 0.5 * d ** -0.5),
        'b_mod': nrm(ks[5], (DEPTH, 6 * d), 0.01),
        'w_in': nrm(ks[6], (DEPTH, d, N_IN), d ** -0.5),
        'w_out': nrm(ks[7], (DEPTH, D_MIX, d), D_MIX ** -0.5 * DEEPNORM_BETA),
        'ln_g': 1.0 + nrm(ks[8], (DEPTH, 2, d), 0.02),
        'ln_b': nrm(ks[9], (DEPTH, 2, d), 0.02),
        'gdn_conv': nrm(ks[10], (DEPTH, GDN_CONV, 3 * GROUP_WIDTH), GDN_CONV ** -0.5),
        'gdn_a_log': jnp.log(jax.random.uniform(ks[11], (DEPTH, 2, GROUP_HEADS), F32, 1.0, 16.0)),
        'gdn_dt_bias': dt + jnp.log(-jnp.expm1(-dt)),
        'gdn_norm': 1.0 + nrm(ks[13], (DEPTH, HEAD_DIM), 0.02),
        'gla_w_gate': nrm(ks[14], (DEPTH, 2, GLA_RANK, GROUP_HEADS * GLA_DK), GLA_RANK ** -0.5),
        'gla_b_gate': nrm(ks[15], (DEPTH, 2, GROUP_HEADS * GLA_DK), 0.01),
        'gla_norm': 1.0 + nrm(ks[16], (DEPTH, HEAD_DIM), 0.02),
        'hgrn_gamma': nrm(ks[17], (2, DEPTH, GROUP_WIDTH), 0.5),
        'hgrn_norm': 1.0 + nrm(ks[18], (DEPTH, HEAD_DIM), 0.02),
        'na_rpb': nrm(ks[19], (DEPTH, GROUP_HEADS, 2 * NA_ROWS - 1, 2 * NA_COLS - 1), 0.1),
        'moe_router': nrm(ks[20], (DEPTH, d, N_EXPERTS), d ** -0.5),
        'moe_w1': nrm(ks[21], (DEPTH, N_EXPERTS, d, D_EXPERT), d ** -0.5),
        'moe_w3': nrm(ks[22], (DEPTH, N_EXPERTS, d, D_EXPERT), d ** -0.5),
        'moe_w2': nrm(ks[23], (DEPTH, N_EXPERTS, D_EXPERT, d), D_EXPERT ** -0.5 * DEEPNORM_BETA),
    }


def reference(x, c, ctx, c_ctx, w_mod, b_mod, w_in, w_out, ln_g, ln_b, gdn_conv, gdn_a_log, gdn_dt_bias,
              gdn_norm, gla_w_gate, gla_b_gate, gla_norm, hgrn_gamma, hgrn_norm, na_rpb,
              moe_router, moe_w1, moe_w3, moe_w2):
    p_lb = jax.nn.softmax(hgrn_gamma.astype(F32), axis=1)
    lower = jnp.cumsum(p_lb, axis=1) - p_lb[:, :1]
    h = ctx
    s_c = jax.nn.silu(c)
    s_cc = jax.nn.silu(c_ctx)
    for l in range(DEPTH):
        keep_ctx = l < DEPTH - 1
        mod = jnp.split((s_c @ w_mod[l] + b_mod[l])[:, None, :], 6, axis=-1)
        mod_c = jnp.split(s_cc @ w_mod[l] + b_mod[l], 6, axis=-1)
        m_ctx, m_lat = hybrid_mixer(modulate(h, mod_c[0], mod_c[1]), modulate(x, mod[0], mod[1]),
                                    w_in[l], gdn_conv[l], gdn_a_log[l], gdn_dt_bias[l], gdn_norm[l],
                                    gla_w_gate[l], gla_b_gate[l], gla_norm[l], lower[:, l], hgrn_norm[l],
                                    na_rpb[l])
        x = layer_norm(DEEPNORM_ALPHA * x + mod[2] * (m_lat @ w_out[l]), ln_g[l, 0], ln_b[l, 0])
        if keep_ctx:
            h = layer_norm(DEEPNORM_ALPHA * h + mod_c[2] * (m_ctx @ w_out[l]), ln_g[l, 0], ln_b[l, 0])
        y = expert_choice_ffn(modulate(x, mod[3], mod[4]), moe_router[l], moe_w1[l], moe_w3[l], moe_w2[l])
        x = layer_norm(DEEPNORM_ALPHA * x + mod[5] * y, ln_g[l, 1], ln_b[l, 1])
        if keep_ctx:
            yc = expert_choice_ffn(modulate(h, mod_c[3], mod_c[4]), moe_router[l], moe_w1[l], moe_w3[l], moe_w2[l])
            h = layer_norm(DEEPNORM_ALPHA * h + mod_c[5] * yc, ln_g[l, 1], ln_b[l, 1])
    return x
```

```python
import functools
import math

import numpy as np
import jax
import jax.numpy as jnp
from jax import lax
from jax.experimental import pallas as pl
from jax.experimental.pallas import tpu as pltpu

F32 = jnp.float32
BF16 = jnp.bfloat16
I32 = jnp.int32

D_MODEL = 2048
DEPTH = 4
GRID_W = 64
HEAD_DIM = 128
N_GROUPS = 4
GROUP_WIDTH = D_MODEL // N_GROUPS
GROUP_HEADS = GROUP_WIDTH // HEAD_DIM
GDN_CONV = 5
CHUNK = 64
GLA_DK = HEAD_DIM // 2
GLA_RANK = 16
GLA_TAU = 16.0
NA_ROWS = 8
NA_COLS = 16
ROPE_BASE = 10000.0
N_EXPERTS = 16
EC_CAPACITY = 2
D_EXPERT = D_MODEL // 2
DEEPNORM_ALPHA = (2 * DEPTH) ** 0.25
LN_EPS = 1e-5
NORM_EPS = 1e-6
NEG_BIG = -1e30

LANES = 128
VMEM_LIMIT = 56 * 1024 * 1024

_OFF = {}
_o = 0
for _name, _w in (("a_q", 512), ("a_k", 512), ("a_v", 512), ("a_z", 512), ("a_beta", 8), ("a_dec", 8),
                  ("b_q", 256), ("b_k", 256), ("b_v", 512), ("b_r", 512), ("b_code", 32),
                  ("c_q", 512), ("c_i", 512), ("c_g", 512), ("c_f", 1024),
                  ("d_q", 512), ("d_k", 512), ("d_v", 512)):
    _OFF[_name] = _o
    _o += _w
N_IN = _o

BLK = dict(a_q=0, a_k=4, a_v=8, a_z=12, c_q=16, c_i=20, c_g=24, c_f0=28, c_f1=32,
           d_q=36, d_k=40, d_v=44, b_v=48, b_r=52, b_q=56, b_k=60, small=64)
NP_BLOCKS = 65
NP = NP_BLOCKS * LANES
SM_BETA, SM_DEC, SM_CODE = 0, 8, 16


def _build_col_index():
    idx = -np.ones((NP,), np.int64)

    def put(blk, src, width):
        idx[blk * LANES: blk * LANES + width] = np.arange(src, src + width)

    for nm in ("a_q", "a_k", "a_v", "a_z", "c_q", "c_i", "c_g", "d_q", "d_k", "d_v", "b_v", "b_r"):
        put(BLK[nm], _OFF[nm], 512)
    put(BLK["c_f0"], _OFF["c_f"], 512)
    put(BLK["c_f1"], _OFF["c_f"] + 512, 512)
    for h in range(GROUP_HEADS):
        put(BLK["b_q"] + h, _OFF["b_q"] + h * GLA_DK, GLA_DK)
        put(BLK["b_k"] + h, _OFF["b_k"] + h * GLA_DK, GLA_DK)
    s = BLK["small"] * LANES
    idx[s + SM_BETA: s + SM_BETA + 8] = np.arange(_OFF["a_beta"], _OFF["a_beta"] + 8)
    idx[s + SM_DEC: s + SM_DEC + 8] = np.arange(_OFF["a_dec"], _OFF["a_dec"] + 8)
    idx[s + SM_CODE: s + SM_CODE + 32] = np.arange(_OFF["b_code"], _OFF["b_code"] + 32)
    return idx


_COL_IDX = _build_col_index()


def _cparams(sem):
    return pltpu.CompilerParams(dimension_semantics=sem, vmem_limit_bytes=VMEM_LIMIT)


def _sigmoid(x):
    return 1.0 / (1.0 + jnp.exp(-x))


def _silu(x):
    return x * _sigmoid(x)


def _split2(x):
    hi = x.astype(BF16)
    lo = (x - hi.astype(F32)).astype(BF16)
    return hi, lo


def _split3(x):
    hi = x.astype(BF16)
    r = x - hi.astype(F32)
    mid = r.astype(BF16)
    lo = (r - mid.astype(F32)).astype(BF16)
    return hi, mid, lo


def _mm(a, b):
    return jnp.dot(a, b, preferred_element_type=F32)


def _mm_nt(a, b):
    return lax.dot_general(a, b, (((1,), (1,)), ((), ())), preferred_element_type=F32)


def _mm_tn(a, b):
    return lax.dot_general(a, b, (((0,), (0,)), ((), ())), preferred_element_type=F32)


def _mm_x2(a, b):
    ah, al = _split2(a)
    bh, bl = _split2(b)
    return _mm(ah, bh) + _mm(ah, bl) + _mm(al, bh)


def _mm_exact_lhs(a_bf, b):
    bh, bm, bl = _split3(b)
    return _mm(a_bf, bh) + _mm(a_bf, bm) + _mm(a_bf, bl)


def _layer_stats(x):
    mu = jnp.mean(x, axis=-1, keepdims=True)
    xc = x - mu
    var = jnp.mean(xc * xc, axis=-1, keepdims=True)
    return xc * lax.rsqrt(var + LN_EPS)


def _mod_kernel(c_ref, w_ref, b_ref, o_ref):
    s = _silu(c_ref[...])
    hi, lo = _split2(s)
    w = w_ref[0].astype(BF16)
    o_ref[0] = _mm(hi, w) + _mm(lo, w) + b_ref[0]


def _modulation(cc, w_mod, b_mod):
    depth, d, n6 = w_mod.shape
    tn = 1024
    return pl.pallas_call(
        _mod_kernel,
        grid=(depth, n6 // tn),
        in_specs=[pl.BlockSpec((8, d), lambda l, j: (0, 0)),
                  pl.BlockSpec((1, d, tn), lambda l, j: (l, 0, j)),
                  pl.BlockSpec((1, 1, tn), lambda l, j: (l, 0, j))],
        out_specs=pl.BlockSpec((1, 8, tn), lambda l, j: (l, 0, j)),
        out_shape=jax.ShapeDtypeStruct((depth, 8, n6), F32),
        compiler_params=_cparams(("arbitrary", "arbitrary")),
        name="modulation",
    )(cc, w_mod, b_mod.reshape(depth, 1, n6))


def _inproj_kernel(x_ref, sh_ref, sc_ref, w_ref, o_ref, u_scr):
    @pl.when(pl.program_id(1) == 0)
    def _():
        y = _layer_stats(x_ref[...]) * (1.0 + sc_ref[0]) + sh_ref[0]
        u_scr[...] = y.astype(BF16)

    o_ref[...] = _mm(u_scr[...], w_ref[...])


def _inproj(x, modrows, w, *, seg_rows, row0):
    n, d = x.shape
    tm = min(512, n, seg_rows)
    tn = 13 * LANES
    tiles_per_seg = seg_rows // tm

    def mrow(k):
        return lambda i, j: ((row0 + i // tiles_per_seg) * 6 + k, 0, 0)

    return pl.pallas_call(
        _inproj_kernel,
        grid=(n // tm, NP // tn),
        in_specs=[pl.BlockSpec((tm, d), lambda i, j: (i, 0)),
                  pl.BlockSpec((1, 1, d), mrow(0)),
                  pl.BlockSpec((1, 1, d), mrow(1)),
                  pl.BlockSpec((d, tn), lambda i, j: (0, j))],
        out_specs=pl.BlockSpec((tm, tn), lambda i, j: (i, j)),
        out_shape=jax.ShapeDtypeStruct((n, NP), F32),
        scratch_shapes=[pltpu.VMEM((tm, d), BF16)],
        compiler_params=_cparams(("arbitrary", "arbitrary")),
        name="inproj",
    )(x, modrows, modrows, w)


def _outproj_kernel(m0, m1, m2, m3, w_ref, x_ref, gate_ref, g_ref, b_ref, o_ref):
    gw = GROUP_WIDTH
    acc = _mm(m0[...], w_ref[0 * gw:1 * gw, :])
    acc += _mm(m1[...], w_ref[1 * gw:2 * gw, :])
    acc += _mm(m2[...], w_ref[2 * gw:3 * gw, :])
    acc += _mm(m3[...], w_ref[3 * gw:4 * gw, :])
    y = DEEPNORM_ALPHA * x_ref[...] + gate_ref[0] * acc
    o_ref[...] = _layer_stats(y) * g_ref[...] + b_ref[...]


def _outproj(mixes, w, x, modrows, g, b, *, seg_rows, row0):
    n, d = x.shape
    tm = min(512, n, seg_rows)
    tiles_per_seg = seg_rows // tm
    mspec = pl.BlockSpec((tm, GROUP_WIDTH), lambda i: (i, 0))
    return pl.pallas_call(
        _outproj_kernel,
        grid=(n // tm,),
        in_specs=[mspec, mspec, mspec, mspec,
                  pl.BlockSpec((d, d), lambda i: (0, 0)),
                  pl.BlockSpec((tm, d), lambda i: (i, 0)),
                  pl.BlockSpec((1, 1, d), lambda i: ((row0 + i // tiles_per_seg) * 6 + 2, 0, 0)),
                  pl.BlockSpec((1, d), lambda i: (0, 0)),
                  pl.BlockSpec((1, d), lambda i: (0, 0))],
        out_specs=pl.BlockSpec((tm, d), lambda i: (i, 0)),
        out_shape=jax.ShapeDtypeStruct((n, d), F32),
        compiler_params=_cparams(("arbitrary",)),
        name="outproj",
    )(*mixes, w, x, modrows, g.reshape(1, d), b.reshape(1, d))


def _bwd_chunk_index(i, n_ctx_chunks, n_chunks):
    return jnp.where(i < n_ctx_chunks, n_ctx_chunks - 1 - i, n_chunks + n_ctx_chunks - 1 - i)


def _rms_gate_store(of_s, ob_s, g_ref, gate_fn, gate_refs, out_refs, seg_bounds):
    for (lo, hi), gate_ref, out_ref in zip(seg_bounds, gate_refs, out_refs):
        n = hi - lo
        tile = min(512, n)
        for r0 in range(0, n, tile):
            o = of_s[lo + r0: lo + r0 + tile, :] + ob_s[lo + r0: lo + r0 + tile, :]
            y = o * lax.rsqrt(jnp.mean(o * o, axis=-1, keepdims=True) + NORM_EPS) * g_ref[...]
            out_ref[r0:r0 + tile, :] = (y * gate_fn(gate_ref[r0:r0 + tile, :])).astype(out_ref.dtype)


def _select_col(x, lane, c):
    return jnp.sum(jnp.where(lane == c, x, 0.0), axis=1, keepdims=True)


def _gdn_kernel(ql, kl, vl, zl, sl, qc, kc, vc, zc, sc, wq, wk, wv, alog, dtb, gn,
                ol, oc, q_s, k_s, v_s, tok_s, pad_s, of_s, ob_s, *, n_ctx, n_lat):
    h = pl.program_id(1)
    n_tot = n_ctx + n_lat
    segs = ((0, n_ctx), (n_ctx, n_tot))

    def conv_into(x_ref, w_ref, dst, lo, n, l2_scale):
        pad_s[0:8, :] = jnp.zeros((8, LANES), F32)
        pad_s[8:8 + n, :] = x_ref[...]
        pad_s[8 + n:16 + n, :] = jnp.zeros((8, LANES), F32)
        tile = min(512, n)
        for r0 in range(0, n, tile):
            acc = jnp.zeros((tile, LANES), F32)
            for i in range(GDN_CONV):
                s0 = 8 + r0 + i - GDN_CONV // 2
                acc = acc + pad_s[s0:s0 + tile, :] * w_ref[i:i + 1, :]
            y = _silu(acc)
            if l2_scale is not None:
                y = y * (lax.rsqrt(jnp.sum(y * y, axis=-1, keepdims=True) + NORM_EPS) * l2_scale)
            dst[lo + r0: lo + r0 + tile, :] = y

    for (lo, hi), (xq, xk, xv) in zip(segs, ((qc, kc, vc), (ql, kl, vl))):
        conv_into(xq, wq, q_s, lo, hi - lo, HEAD_DIM ** -0.5)
        conv_into(xk, wk, k_s, lo, hi - lo, 1.0)
        conv_into(xv, wv, v_s, lo, hi - lo, None)

    def tok_into(s_ref, lo, n):
        tile = min(512, n)
        lane = lax.broadcasted_iota(I32, (tile, LANES), 1)
        for r0 in range(0, n, tile):
            sm = s_ref[r0:r0 + tile, :]
            cols = []
            for d in range(2):
                beta = _sigmoid(_select_col(sm, lane, SM_BETA + d * GROUP_HEADS + h))
                dec = _select_col(sm, lane, SM_DEC + d * GROUP_HEADS + h)
                a_neg = -jnp.exp(_select_col(alog[...], lane[0:1], d * GROUP_HEADS + h))
                bias = _select_col(dtb[...], lane[0:1], d * GROUP_HEADS + h)
                la = a_neg * jax.nn.softplus(dec + bias)
                cols += [beta, la]
            t = jnp.where(lane == 0, cols[0], jnp.where(lane == 1, cols[1],
                          jnp.where(lane == 2, cols[2], jnp.where(lane == 3, cols[3], 0.0))))
            tok_s[lo + r0: lo + r0 + tile, :] = t

    tok_into(sc, 0, n_ctx)
    tok_into(sl, n_ctx, n_lat)

    c = CHUNK
    r = lax.broadcasted_iota(I32, (c, c), 0)
    cc = lax.broadcasted_iota(I32, (c, c), 1)
    eye = r == cc
    eye_f = eye.astype(F32)
    ones_bf = jnp.ones((c, c), BF16)

    def inv_unit(a):
        t = eye_f - a
        p = a
        for _ in range(5):
            p = _mm_x2(p, p)
            t = t + _mm_x2(t, p)
        return t

    def chunk(off, s, rev):
        q = q_s[pl.ds(off, c), :]
        k = k_s[pl.ds(off, c), :]
        v = v_s[pl.ds(off, c), :]
        tk = tok_s[pl.ds(off, c), :]
        beta = tk[:, 2:3] if rev else tk[:, 0:1]
        la = tk[:, 3:4] if rev else tk[:, 1:2]
        tri = (cc >= r) if rev else (cc <= r)
        gi_full = _mm_exact_lhs(tri.astype(BF16), jnp.broadcast_to(la, (c, LANES)))
        gi = gi_full[:, :c]
        gj = _mm_exact_lhs(ones_bf, jnp.where(eye, gi, 0.0))
        diff = gi - gj
        strict = (cc > r) if rev else (cc < r)
        incl = (cc >= r) if rev else (cc <= r)
        d_strict = jnp.exp(jnp.where(strict, diff, -jnp.inf))
        d_incl = jnp.exp(jnp.where(incl, diff, -jnp.inf))
        qb = q.astype(BF16)
        kb = k.astype(BF16)
        a = beta * _mm_nt(kb, kb) * d_strict
        p = _mm_nt(qb, kb) * d_incl
        eg = jnp.exp(gi_full)
        rhs = jnp.concatenate([beta * v, (beta * eg) * k], axis=1)
        sol = _mm_x2(inv_unit(a), rhs)
        u = sol[:, :LANES]
        w = sol[:, LANES:]
        g_last = gi_full[0:1, :] if rev else gi_full[c - 1:c, :]
        qg = q * eg
        kd = k * jnp.exp(g_last - gi_full)
        sb = s.astype(BF16)
        v_new = u - _mm(w.astype(BF16), sb)
        vb = v_new.astype(BF16)
        o = _mm(qg.astype(BF16), sb) + _mm(p.astype(BF16), vb)
        s = jnp.exp(g_last) * s + _mm_tn(kd.astype(BF16), vb)
        return o, s

    n_cc = n_ctx // c
    n_ch = n_tot // c

    def step(i, carry):
        sf, sb_ = carry
        off_f = pl.multiple_of(i * c, c)
        off_b = pl.multiple_of(_bwd_chunk_index(i, n_cc, n_ch) * c, c)
        o_f, sf = chunk(off_f, sf, False)
        o_b, sb_ = chunk(off_b, sb_, True)
        of_s[pl.ds(off_f, c), :] = o_f
        ob_s[pl.ds(off_b, c), :] = o_b
        return sf, sb_

    z0 = jnp.zeros((HEAD_DIM, HEAD_DIM), F32)
    lax.fori_loop(0, n_ch, step, (z0, z0))

    _rms_gate_store(of_s, ob_s, gn, _silu, (zc, zl), (oc, ol), segs)


def _gdn(p_lat, p_ctx, conv_w, a_log, dt_bias, gnorm, *, batch):
    n_lat = p_lat.shape[0] // batch
    n_ctx = p_ctx.shape[0] // batch
    n_tot = n_lat + n_ctx

    def blk(n, base):
        return pl.BlockSpec((n, LANES), lambda b, h: (b, base + h))

    def small(n):
        return pl.BlockSpec((n, LANES), lambda b, h: (b, BLK["small"]))

    def cw(base):
        return pl.BlockSpec((8, LANES), lambda b, h: (0, base + h))

    row = pl.BlockSpec((1, LANES), lambda b, h: (0, 0))
    conv_p = jnp.zeros((8, 3 * GROUP_WIDTH), F32).at[:GDN_CONV].set(conv_w)
    alog_row = jnp.zeros((1, LANES), F32).at[0, :2 * GROUP_HEADS].set(a_log.reshape(-1))
    dtb_row = jnp.zeros((1, LANES), F32).at[0, :2 * GROUP_HEADS].set(dt_bias.reshape(-1))
    lat_in = [blk(n_lat, BLK[k]) for k in ("a_q", "a_k", "a_v", "a_z")] + [small(n_lat)]
    ctx_in = [blk(n_ctx, BLK[k]) for k in ("a_q", "a_k", "a_v", "a_z")] + [small(n_ctx)]
    kern = functools.partial(_gdn_kernel, n_ctx=n_ctx, n_lat=n_lat)
    return pl.pallas_call(
        kern,
        grid=(batch, GROUP_HEADS),
        in_specs=lat_in + ctx_in + [cw(0), cw(4), cw(8), row, row, row],
        out_specs=[pl.BlockSpec((n_lat, LANES), lambda b, h: (b, h)),
                   pl.BlockSpec((n_ctx, LANES), lambda b, h: (b, h))],
        out_shape=[jax.ShapeDtypeStruct((batch * n_lat, GROUP_WIDTH), BF16),
                   jax.ShapeDtypeStruct((batch * n_ctx, GROUP_WIDTH), BF16)],
        scratch_shapes=[pltpu.VMEM((n_tot, LANES), F32)] * 4
        + [pltpu.VMEM((n_lat + 16, LANES), F32)]
        + [pltpu.VMEM((n_tot, LANES), F32)] * 2,
        compiler_params=_cparams(("arbitrary", "arbitrary")),
        name="gdn",
    )(*([p_lat] * 5), *([p_ctx] * 5), conv_p, conv_p, conv_p, alog_row, dtb_row, gnorm.reshape(1, LANES))


N_LEVELS = 6


def _level_matrices():
    c = CHUNK
    i = np.arange(c)[:, None]
    u = np.arange(c)[None, :]
    mats = []
    for lv in range(N_LEVELS + 1):
        s = 1 << lv
        start = (i // s) * s
        end = start + s - 1
        mats.append(((u >= start) & (u <= i)).astype(np.float32))
        mats.append(((u > i) & (u <= end)).astype(np.float32))
    fwd = np.concatenate(mats, axis=0)
    bwd = np.concatenate([m[::-1, ::-1] for m in mats], axis=0)
    return np.stack([fwd, bwd])


_LEVEL_MATS = _level_matrices()


def _diag_scan(q_s, kf_s, kb_s, v_s, laf_s, lab_s, of_s, ob_s, wall_ref, n_ctx, n_tot):
    c = CHUNK
    r = lax.broadcasted_iota(I32, (c, c), 0)
    cc = lax.broadcasted_iota(I32, (c, c), 1)
    eye_f = (r == cc).astype(F32)
    masks = []
    for rev in (False, True):
        ri = (c - 1 - r) if rev else r
        ci = (c - 1 - cc) if rev else cc
        ms = []
        for lv in range(N_LEVELS):
            bi = ri >> lv
            bj = ci >> lv
            ms.append((((bi & 1) == 1) & (bj == bi - 1)).astype(F32))
        masks.append(ms)

    def chunk(off, st, rev):
        q = q_s[pl.ds(off, c), :]
        k = (kb_s if rev else kf_s)[pl.ds(off, c), :]
        v = v_s[pl.ds(off, c), :]
        la = (lab_s if rev else laf_s)[pl.ds(off, c), :]
        h1, h2, h3 = _split3(la)
        e = _mm(wall_ref[1 if rev else 0], jnp.concatenate([h1, h2, h3], axis=0))
        p = eye_f * _mm_nt(q.astype(BF16), k.astype(BF16))
        for lv in range(N_LEVELS):
            qs = (q * jnp.exp(e[2 * lv * c:(2 * lv + 1) * c, :])).astype(BF16)
            ks = (k * jnp.exp(e[(2 * lv + 1) * c:(2 * lv + 2) * c, :])).astype(BF16)
            p = p + masks[1 if rev else 0][lv] * _mm_nt(qs, ks)
        top = 2 * N_LEVELS * c
        e_q = e[top:top + c, :]
        qg = (q * jnp.exp(e_q)).astype(BF16)
        kd = (k * jnp.exp(e[top + c:top + 2 * c, :])).astype(BF16)
        a_end = jnp.exp(e_q[0:1, :] if rev else e_q[c - 1:c, :])
        vb = v.astype(BF16)
        o = _mm_nt(qg, st.astype(BF16)) + _mm(p.astype(BF16), vb)
        st = st * a_end + _mm_tn(vb, kd)
        return o, st

    n_cc = n_ctx // c
    n_ch = n_tot // c

    def step(i, carry):
        sf, sb_ = carry
        off_f = pl.multiple_of(i * c, c)
        off_b = pl.multiple_of(_bwd_chunk_index(i, n_cc, n_ch) * c, c)
        o_f, sf = chunk(off_f, sf, False)
        o_b, sb_ = chunk(off_b, sb_, True)
        of_s[pl.ds(off_f, c), :] = o_f
        ob_s[pl.ds(off_b, c), :] = o_b
        return sf, sb_

    z0 = jnp.zeros((HEAD_DIM, HEAD_DIM), F32)
    lax.fori_loop(0, n_ch, step, (z0, z0))


def _gla_kernel(ql, kl, vl, rl, sl, qc, kc, vc, rc, sc, cos_ref, sin_ref, perm_ref, wg_ref, bg_ref, gn,
                wall_ref, ol, oc, q_s, k_s, v_s, laf_s, lab_s, of_s, ob_s, *, n_ctx, n_lat):
    n_tot = n_ctx + n_lat
    segs = ((0, n_ctx), (n_ctx, n_tot))
    for (lo, hi), (xq, xk, xv, xs), rope in zip(segs, ((qc, kc, vc, sc), (ql, kl, vl, sl)), (False, True)):
        n = hi - lo
        tile = min(512, n)
        for r0 in range(0, n, tile):
            sl_ = slice(r0, r0 + tile)
            q = xq[sl_, :]
            k = xk[sl_, :]
            if rope:
                cs = cos_ref[sl_, :]
                sn = sin_ref[sl_, :]
                q = q * cs + _mm_exact_rhs(q, perm_ref[...]) * sn
                k = k * cs + _mm_exact_rhs(k, perm_ref[...]) * sn
            q_s[lo + r0: lo + r0 + tile, :] = q * (GLA_DK ** -0.5)
            k_s[lo + r0: lo + r0 + tile, :] = k
            v_s[lo + r0: lo + r0 + tile, :] = xv[sl_, :]
            sm = xs[sl_, :]
            for d, dst in ((0, laf_s), (1, lab_s)):
                gate = _mm_x2(sm, wg_ref[d, 0]) + bg_ref[d, 0]
                dst[lo + r0: lo + r0 + tile, :] = jax.nn.log_sigmoid(gate) * (1.0 / GLA_TAU)

    _diag_scan(q_s, k_s, k_s, v_s, laf_s, lab_s, of_s, ob_s, wall_ref, n_ctx, n_tot)
    _rms_gate_store(of_s, ob_s, gn, _silu, (rc, rl), (oc, ol), segs)


def _mm_exact_rhs(a, b_bf):
    ah, am, al = _split3(a)
    return _mm(ah, b_bf) + _mm(am, b_bf) + _mm(al, b_bf)


def _rope_tables(n_lat):
    t = np.arange(n_lat)
    nf = GLA_DK // 4
    inv = (ROPE_BASE ** (-jnp.arange(nf, dtype=F32) / nf))
    lane = np.arange(LANES)
    f = lane % nf
    use_col = (lane % GLA_DK) >= GLA_DK // 2
    first = (lane % (2 * nf)) < nf
    real = lane < GLA_DK
    pos = jnp.where(use_col[None, :], (t % GRID_W)[:, None], (t // GRID_W)[:, None]).astype(F32)
    ang = pos * inv[f][None, :]
    cos = jnp.where(real[None, :], jnp.cos(ang), 1.0)
    sin = jnp.where(real[None, :], jnp.where(first[None, :], -jnp.sin(ang), jnp.sin(ang)), 0.0)
    partner = np.where(first, lane + nf, lane - nf)
    perm = np.zeros((LANES, LANES), np.float32)
    perm[partner[real], lane[real]] = 1.0
    return cos.astype(F32), sin.astype(F32), jnp.asarray(perm, BF16)


def _gla(p_lat, p_ctx, w_gate, b_gate, gnorm, wall, *, batch):
    n_lat = p_lat.shape[0] // batch
    n_ctx = p_ctx.shape[0] // batch
    n_tot = n_lat + n_ctx
    cos, sin, perm = _rope_tables(n_lat)
    wg = jnp.zeros((2, GROUP_HEADS, LANES, LANES), F32)
    wsrc = w_gate.reshape(2, GLA_RANK, GROUP_HEADS, GLA_DK).transpose(0, 2, 1, 3)
    for d in range(2):
        wg = wg.at[d, :, SM_CODE + d * GLA_RANK: SM_CODE + (d + 1) * GLA_RANK, :GLA_DK].set(wsrc[d])
    bg = jnp.zeros((2, GROUP_HEADS, 1, LANES), F32).at[:, :, 0, :GLA_DK].set(
        b_gate.reshape(2, GROUP_HEADS, GLA_DK))

    def blk(n, base):
        return pl.BlockSpec((n, LANES), lambda b, h: (b, base + h))

    def small(n):
        return pl.BlockSpec((n, LANES), lambda b, h: (b, BLK["small"]))

    def const2(shape):
        return pl.BlockSpec(shape, lambda b, h: (0, 0))

    names = ("b_q", "b_k", "b_v", "b_r")
    kern = functools.partial(_gla_kernel, n_ctx=n_ctx, n_lat=n_lat)
    return pl.pallas_call(
        kern,
        grid=(batch, GROUP_HEADS),
        in_specs=[blk(n_lat, BLK[k]) for k in names] + [small(n_lat)]
        + [blk(n_ctx, BLK[k]) for k in names] + [small(n_ctx)]
        + [const2((n_lat, LANES)), const2((n_lat, LANES)), const2((LANES, LANES)),
           pl.BlockSpec((2, 1, LANES, LANES), lambda b, h: (0, h, 0, 0)),
           pl.BlockSpec((2, 1, 1, LANES), lambda b, h: (0, h, 0, 0)),
           const2((1, LANES)),
           pl.BlockSpec(wall.shape, lambda b, h: (0, 0, 0))],
        out_specs=[pl.BlockSpec((n_lat, LANES), lambda b, h: (b, h)),
                   pl.BlockSpec((n_ctx, LANES), lambda b, h: (b, h))],
        out_shape=[jax.ShapeDtypeStruct((batch * n_lat, GROUP_WIDTH), BF16),
                   jax.ShapeDtypeStruct((batch * n_ctx, GROUP_WIDTH), BF16)],
        scratch_shapes=[pltpu.VMEM((n_tot, LANES), F32)] * 7,
        compiler_params=_cparams(("arbitrary", "arbitrary")),
        name="gla",
    )(*([p_lat] * 5), *([p_ctx] * 5), cos, sin, perm, wg, bg, gnorm.reshape(1, LANES), wall)


def _hgrn_kernel(ql, il, gl, f0l, f1l, qc, ic, gc, f0c, f1c, gam_ref, gn, wall_ref,
                 ol, oc, q_s, kf_s, kb_s, v_s, laf_s, lab_s, of_s, ob_s, *, n_ctx, n_lat, layer):
    n_tot = n_ctx + n_lat
    segs = ((0, n_ctx), (n_ctx, n_tot))
    lbs = []
    for d in range(2):
        gam = gam_ref[d]
        ex = jnp.exp(gam - jnp.max(gam, axis=0, keepdims=True))
        pr = ex / jnp.sum(ex, axis=0, keepdims=True)
        lb = jnp.zeros((1, LANES), F32)
        for m in range(1, layer + 1):
            lb = lb + pr[m:m + 1, :]
        lbs.append(lb)
    for (lo, hi), (xq, xi, xf0, xf1) in zip(segs, ((qc, ic, f0c, f1c), (ql, il, f0l, f1l))):
        n = hi - lo
        tile = min(512, n)
        for r0 in range(0, n, tile):
            sl_ = slice(r0, r0 + tile)
            dl = slice(lo + r0, lo + r0 + tile)
            q_s[dl, :] = xq[sl_, :]
            v_s[dl, :] = xi[sl_, :]
            for d, xf, k_dst, la_dst in ((0, xf0, kf_s, laf_s), (1, xf1, kb_s, lab_s)):
                f = lbs[d] + (1.0 - lbs[d]) * _sigmoid(xf[sl_, :])
                k_dst[dl, :] = 1.0 - f
                la_dst[dl, :] = jnp.log(f)

    _diag_scan(q_s, kf_s, kb_s, v_s, laf_s, lab_s, of_s, ob_s, wall_ref, n_ctx, n_tot)
    _rms_gate_store(of_s, ob_s, gn, _sigmoid, (gc, gl), (oc, ol), segs)


def _hgrn(p_lat, p_ctx, gamma, gnorm, wall, *, batch, layer):
    n_lat = p_lat.shape[0] // batch
    n_ctx = p_ctx.shape[0] // batch
    n_tot = n_lat + n_ctx

    def blk(n, base):
        return pl.BlockSpec((n, LANES), lambda b, h: (b, base + h))

    names = ("c_q", "c_i", "c_g", "c_f0", "c_f1")
    kern = functools.partial(_hgrn_kernel, n_ctx=n_ctx, n_lat=n_lat, layer=layer)
    return pl.pallas_call(
        kern,
        grid=(batch, GROUP_HEADS),
        in_specs=[blk(n_lat, BLK[k]) for k in names] + [blk(n_ctx, BLK[k]) for k in names]
        + [pl.BlockSpec((2, gamma.shape[1], LANES), lambda b, h: (0, 0, h)),
           pl.BlockSpec((1, LANES), lambda b, h: (0, 0)),
           pl.BlockSpec(wall.shape, lambda b, h: (0, 0, 0))],
        out_specs=[pl.BlockSpec((n_lat, LANES), lambda b, h: (b, h)),
                   pl.BlockSpec((n_ctx, LANES), lambda b, h: (b, h))],
        out_shape=[jax.ShapeDtypeStruct((batch * n_lat, GROUP_WIDTH), BF16),
                   jax.ShapeDtypeStruct((batch * n_ctx, GROUP_WIDTH), BF16)],
        scratch_shapes=[pltpu.VMEM((n_tot, LANES), F32)] * 8,
        compiler_params=_cparams(("arbitrary", "arbitrary")),
        name="hgrn",
    )(*([p_lat] * 5), *([p_ctx] * 5), gamma, gnorm.reshape(1, LANES), wall)


def _na_kernel(ql, kl, vl, qc, kc, vc, bias_ref, ol, oc, *, n_rows):
    scale = HEAD_DIM ** -0.5
    kcb = kc[...].astype(BF16)
    vcb = vc[...].astype(BF16)
    w = GRID_W
    kr = min(NA_ROWS, n_rows)

    def row(rr, _):
        rs = jnp.clip(rr - kr // 2, 0, n_rows - kr)
        q = ql[pl.ds(pl.multiple_of(rr * w, w), w), :].astype(BF16)
        koff = pl.multiple_of(rs * w, w)
        kb = kl[pl.ds(koff, kr * w), :].astype(BF16)
        vb = vl[pl.ds(koff, kr * w), :].astype(BF16)
        s_loc = _mm_nt(q, kb) * scale + bias_ref[0, rr - rs]
        s_ctx = _mm_nt(q, kcb) * scale
        m = jnp.maximum(jnp.max(s_loc, axis=-1, keepdims=True), jnp.max(s_ctx, axis=-1, keepdims=True))
        p_loc = jnp.exp(s_loc - m)
        p_ctx = jnp.exp(s_ctx - m)
        den = jnp.sum(p_loc, axis=-1, keepdims=True) + jnp.sum(p_ctx, axis=-1, keepdims=True)
        o = _mm(p_loc.astype(BF16), vb) + _mm(p_ctx.astype(BF16), vcb)
        ol[pl.ds(pl.multiple_of(rr * w, w), w), :] = (o / den).astype(ol.dtype)
        return 0

    lax.fori_loop(0, n_rows, row, 0)

    s = _mm_nt(qc[...].astype(BF16), kcb) * scale
    p = jnp.exp(s - jnp.max(s, axis=-1, keepdims=True))
    o = _mm(p.astype(BF16), vcb) / jnp.sum(p, axis=-1, keepdims=True)
    oc[...] = o.astype(oc.dtype)


def _na_bias(rpb, n_rows):
    kr = min(NA_ROWS, n_rows)
    off = np.arange(kr)
    j = np.arange(kr)
    cq = np.arange(GRID_W)
    dr = j[None, :] - off[:, None] + NA_ROWS - 1
    dr_ok = (dr >= 0) & (dr < 2 * NA_ROWS - 1)
    dc = np.clip(cq[None, :] - cq[:, None], -(NA_COLS - 1), NA_COLS - 1) + NA_COLS - 1
    col_start = np.clip(cq - NA_COLS // 2, 0, GRID_W - NA_COLS)
    col_in = (cq[None, :] >= col_start[:, None]) & (cq[None, :] < col_start[:, None] + NA_COLS)
    b = rpb.astype(F32)[:, np.clip(dr, 0, 2 * NA_ROWS - 2)[:, None, :, None], dc[None, :, None, :]]
    ok = dr_ok[:, None, :, None] & col_in[None, :, None, :]
    b = jnp.where(ok[None], b, NEG_BIG)
    return b.reshape(rpb.shape[0], kr, GRID_W, kr * GRID_W)


def _na(p_lat, p_ctx, rpb, *, batch):
    n_lat = p_lat.shape[0] // batch
    n_ctx = p_ctx.shape[0] // batch
    n_rows = n_lat // GRID_W
    bias = _na_bias(rpb, n_rows)

    def blk(n, base):
        return pl.BlockSpec((n, LANES), lambda b, h: (b, base + h))

    names = ("d_q", "d_k", "d_v")
    return pl.pallas_call(
        functools.partial(_na_kernel, n_rows=n_rows),
        grid=(batch, GROUP_HEADS),
        in_specs=[blk(n_lat, BLK[k]) for k in names] + [blk(n_ctx, BLK[k]) for k in names]
        + [pl.BlockSpec((1,) + bias.shape[1:], lambda b, h: (h, 0, 0, 0))],
        out_specs=[pl.BlockSpec((n_lat, LANES), lambda b, h: (b, h)),
                   pl.BlockSpec((n_ctx, LANES), lambda b, h: (b, h))],
        out_shape=[jax.ShapeDtypeStruct((batch * n_lat, GROUP_WIDTH), BF16),
                   jax.ShapeDtypeStruct((batch * n_ctx, GROUP_WIDTH), BF16)],
        compiler_params=_cparams(("arbitrary", "arbitrary")),
        name="na",
    )(*([p_lat] * 3), *([p_ctx] * 3), bias)


def _router_kernel(x_ref, sh_ref, sc_ref, wr_ref, u_ref, code_ref, gate_ref, lg_scr, *, cap, n_tiles, tile):
    j = pl.program_id(1)
    u = _layer_stats(x_ref[...]) * (1.0 + sc_ref[0]) + sh_ref[0]
    u_ref[...] = u.astype(BF16)
    lg_scr[j] = lax.dot_general(wr_ref[...], u, (((1,), (1,)), ((), ())),
                                precision=lax.Precision.HIGHEST, preferred_element_type=F32)

    @pl.when(j == n_tiles - 1)
    def _():
        lg = lg_scr[...]
        ex = jnp.exp(lg - jnp.max(lg, axis=1, keepdims=True))
        aff = ex / jnp.sum(ex, axis=1, keepdims=True)
        bits = lax.bitcast_convert_type(aff, I32)

        def count(mask):
            per = jnp.sum(mask.astype(F32), axis=0)
            return jnp.sum(per, axis=1, keepdims=True)

        def bis(it, thr):
            cand = thr | jnp.left_shift(jnp.int32(1), 30 - it)
            ok = count(bits >= cand[None]) >= float(cap)
            return jnp.where(ok, cand, thr)

        thr = lax.fori_loop(0, 31, bis, jnp.zeros((N_EXPERTS, 1), I32))
        gt = bits > thr[None]
        eq = bits == thr[None]
        need = float(cap) - count(gt)
        r = lax.broadcasted_iota(I32, (tile, tile), 0)
        c = lax.broadcasted_iota(I32, (tile, tile), 1)
        upper = (r < c).astype(BF16)
        carry_eq = jnp.zeros((N_EXPERTS, 1), F32)
        carry_sel = jnp.zeros((N_EXPERTS, 1), F32)
        for t in range(n_tiles):
            eq_t = eq[t].astype(BF16)
            pre_eq = _mm(eq_t, upper) + carry_eq
            sel = gt[t] | (eq[t] & (pre_eq < need))
            sel_b = sel.astype(BF16)
            pos = _mm(sel_b, upper) + carry_sel
            code_ref[0, t] = jnp.where(sel, pos.astype(I32), -1)
            gate_ref[0, t] = aff[t]
            carry_eq = carry_eq + jnp.sum(eq_t.astype(F32), axis=1, keepdims=True)
            carry_sel = carry_sel + jnp.sum(sel_b.astype(F32), axis=1, keepdims=True)


def _router(x, modrows, w_router, *, seg_rows, row0, row_stride):
    n, d = x.shape
    nseg = n // seg_rows
    tile = min(512, seg_rows)
    n_tiles = seg_rows // tile
    cap = EC_CAPACITY * seg_rows // N_EXPERTS
    kern = functools.partial(_router_kernel, cap=cap, n_tiles=n_tiles, tile=tile)
    return pl.pallas_call(
        kern,
        grid=(nseg, n_tiles),
        in_specs=[pl.BlockSpec((tile, d), lambda s, j: (s * n_tiles + j, 0)),
                  pl.BlockSpec((1, 1, d), lambda s, j: ((row0 + s * row_stride) * 6 + 3, 0, 0)),
                  pl.BlockSpec((1, 1, d), lambda s, j: ((row0 + s * row_stride) * 6 + 4, 0, 0)),
                  pl.BlockSpec((N_EXPERTS, d), lambda s, j: (0, 0))],
        out_specs=[pl.BlockSpec((tile, d), lambda s, j: (s * n_tiles + j, 0)),
                   pl.BlockSpec((1, n_tiles, N_EXPERTS, tile), lambda s, j: (s, 0, 0, 0)),
                   pl.BlockSpec((1, n_tiles, N_EXPERTS, tile), lambda s, j: (s, 0, 0, 0))],
        out_shape=[jax.ShapeDtypeStruct((n, d), BF16),
                   jax.ShapeDtypeStruct((nseg, n_tiles, N_EXPERTS, tile), I32),
                   jax.ShapeDtypeStruct((nseg, n_tiles, N_EXPERTS, tile), F32)],
        scratch_shapes=[pltpu.VMEM((n_tiles, N_EXPERTS, tile), F32)],
        compiler_params=_cparams(("arbitrary", "arbitrary")),
        name="router",
    )(x, modrows, modrows, w_router.T)


def _gather_kernel(u_ref, code_ref, o_ref, *, cap, n_tiles, tile):
    slot = lax.broadcasted_iota(I32, (cap, tile), 0)
    acc = jnp.zeros((cap, u_ref.shape[1]), F32)
    for t in range(n_tiles):
        oh = (code_ref[0, 0, t:t + 1, :] == slot).astype(BF16)
        acc = acc + _mm(oh, u_ref[t * tile:(t + 1) * tile, :])
    o_ref[0, 0] = acc.astype(BF16)


def _gather(u, code_e, *, seg_rows):
    n, d = u.shape
    nseg, _, n_tiles, tile = code_e.shape
    cap = EC_CAPACITY * seg_rows // N_EXPERTS
    return pl.pallas_call(
        functools.partial(_gather_kernel, cap=cap, n_tiles=n_tiles, tile=tile),
        grid=(nseg, N_EXPERTS),
        in_specs=[pl.BlockSpec((seg_rows, d), lambda s, e: (s, 0)),
                  pl.BlockSpec((1, 1, n_tiles, tile), lambda s, e: (s, e, 0, 0))],
        out_specs=pl.BlockSpec((1, 1, cap, d), lambda s, e: (e, s, 0, 0)),
        out_shape=jax.ShapeDtypeStruct((N_EXPERTS, nseg, cap, d), BF16),
        compiler_params=_cparams(("arbitrary", "arbitrary")),
        name="moe_gather",
    )(u, code_e)


def _ffn_kernel(x_ref, w1_ref, w3_ref, w2_ref, o_ref):
    x = x_ref[0]
    a = _mm(x, w1_ref[0])
    g = _mm(x, w3_ref[0])
    hid = (_silu(a) * g).astype(BF16)
    o_ref[0] = _mm(hid, w2_ref[0]).astype(BF16)


def _ffn(xs, w1, w3, w2):
    e, r, d = xs.shape
    f = w1.shape[2]
    tr = min(512, r)
    return pl.pallas_call(
        _ffn_kernel,
        grid=(e, r // tr),
        in_specs=[pl.BlockSpec((1, tr, d), lambda i, j: (i, j, 0)),
                  pl.BlockSpec((1, d, f), lambda i, j: (i, 0, 0)),
                  pl.BlockSpec((1, d, f), lambda i, j: (i, 0, 0)),
                  pl.BlockSpec((1, f, d), lambda i, j: (i, 0, 0))],
        out_specs=pl.BlockSpec((1, tr, d), lambda i, j: (i, j, 0)),
        out_shape=jax.ShapeDtypeStruct((e, r, d), BF16),
        compiler_params=_cparams(("arbitrary", "arbitrary")),
        name="moe_ffn",
    )(xs, w1, w3, w2)


def _combine_kernel(y_ref, code_ref, gate_ref, x_ref, m5_ref, g_ref, b_ref, o_ref, acc, *, cap):
    e = pl.program_id(2)

    @pl.when(e == 0)
    def _():
        acc[...] = jnp.zeros_like(acc)

    lane = lax.broadcasted_iota(I32, code_ref.shape[1:], 1)
    code = jnp.sum(jnp.where(lane == e, code_ref[0], 0), axis=1, keepdims=True)
    gate = jnp.sum(jnp.where(lane == e, gate_ref[0], 0.0), axis=1, keepdims=True)
    slot = lax.broadcasted_iota(I32, (code.shape[0], cap), 1)
    wm = jnp.where(code == slot, gate, 0.0).astype(BF16)
    acc[...] += _mm(wm, y_ref[0, 0])

    @pl.when(e == N_EXPERTS - 1)
    def _():
        y = DEEPNORM_ALPHA * x_ref[...] + m5_ref[0] * acc[...]
        o_ref[...] = _layer_stats(y) * g_ref[...] + b_ref[...]


def _combine(ys, code_t, gate_t, x, modrows, g, b, *, seg_rows, row0, row_stride):
    n, d = x.shape
    nseg = n // seg_rows
    cap = ys.shape[2]
    tj = min(512, seg_rows)
    nt = seg_rows // tj
    return pl.pallas_call(
        functools.partial(_combine_kernel, cap=cap),
        grid=(nseg, nt, N_EXPERTS),
        in_specs=[pl.BlockSpec((1, 1, cap, d), lambda s, j, e: (e, s, 0, 0)),
                  pl.BlockSpec((1, tj, N_EXPERTS), lambda s, j, e: (s, j, 0)),
                  pl.BlockSpec((1, tj, N_EXPERTS), lambda s, j, e: (s, j, 0)),
                  pl.BlockSpec((tj, d), lambda s, j, e: (s * nt + j, 0)),
                  pl.BlockSpec((1, 1, d), lambda s, j, e: ((row0 + s * row_stride) * 6 + 5, 0, 0)),
                  pl.BlockSpec((1, d), lambda s, j, e: (0, 0)),
                  pl.BlockSpec((1, d), lambda s, j, e: (0, 0))],
        out_specs=pl.BlockSpec((tj, d), lambda s, j, e: (s * nt + j, 0)),
        out_shape=jax.ShapeDtypeStruct((n, d), F32),
        scratch_shapes=[pltpu.VMEM((tj, d), F32)],
        compiler_params=_cparams(("arbitrary", "arbitrary", "arbitrary")),
        name="moe_combine",
    )(ys, code_t, gate_t, x, modrows, g.reshape(1, d), b.reshape(1, d))


def _moe(x, modrows, w_router, w1, w3, w2, g, b, *, seg_rows, row0, row_stride):
    n, d = x.shape
    nseg = n // seg_rows
    u, code, gate = _router(x, modrows, w_router, seg_rows=seg_rows, row0=row0, row_stride=row_stride)
    code_e = code.transpose(0, 2, 1, 3)
    code_t = code.transpose(0, 1, 3, 2).reshape(nseg, seg_rows, N_EXPERTS)
    gate_t = gate.transpose(0, 1, 3, 2).reshape(nseg, seg_rows, N_EXPERTS)
    xs = _gather(u, code_e, seg_rows=seg_rows)
    cap = xs.shape[2]
    ys = _ffn(xs.reshape(N_EXPERTS, nseg * cap, d), w1, w3, w2).reshape(N_EXPERTS, nseg, cap, d)
    return _combine(ys, code_t, gate_t, x, modrows, g, b, seg_rows=seg_rows, row0=row0,
                    row_stride=row_stride)


def kernel(x, c, ctx, c_ctx, w_mod, b_mod, w_in, w_out, ln_g, ln_b, gdn_conv, gdn_a_log, gdn_dt_bias,
           gdn_norm, gla_w_gate, gla_b_gate, gla_norm, hgrn_gamma, hgrn_norm, na_rpb,
           moe_router, moe_w1, moe_w3, moe_w2):
    batch, seq, d = x.shape
    n_ctx = ctx.shape[1]
    xl = x.reshape(batch * seq, d)
    xc = ctx.reshape(batch * n_ctx, d)

    cc = jnp.zeros((8, d), F32).at[:batch].set(c).at[batch].set(c_ctx)
    mods = _modulation(cc, w_mod, b_mod)

    col = jnp.asarray(np.maximum(_COL_IDX, 0))
    keep = jnp.asarray(_COL_IDX >= 0)
    w_in_r = jnp.where(keep[None, None, :], jnp.take(w_in, col, axis=2), 0.0).astype(BF16)
    w_out_b = w_out.astype(BF16)
    w1_b, w3_b, w2_b = moe_w1.astype(BF16), moe_w3.astype(BF16), moe_w2.astype(BF16)
    wall = jnp.asarray(np.concatenate([_LEVEL_MATS] * 3, axis=2), BF16)

    for l in range(DEPTH):
        keep_ctx = l < DEPTH - 1
        modrows = mods[l].reshape(8 * 6, 1, d)
        p_lat = _inproj(xl, modrows, w_in_r[l], seg_rows=seq, row0=0)
        p_ctx = _inproj(xc, modrows, w_in_r[l], seg_rows=batch * n_ctx, row0=batch)
        mixes = [
            _gdn(p_lat, p_ctx, gdn_conv[l], gdn_a_log[l], gdn_dt_bias[l], gdn_norm[l], batch=batch),
            _gla(p_lat, p_ctx, gla_w_gate[l], gla_b_gate[l], gla_norm[l], wall, batch=batch),
            _hgrn(p_lat, p_ctx, hgrn_gamma, hgrn_norm[l], wall, batch=batch, layer=l),
            _na(p_lat, p_ctx, na_rpb[l], batch=batch),
        ]
        xl = _outproj([m[0] for m in mixes], w_out_b[l], xl, modrows, ln_g[l, 0], ln_b[l, 0],
                      seg_rows=seq, row0=0)
        if keep_ctx:
            xc = _outproj([m[1] for m in mixes], w_out_b[l], xc, modrows, ln_g[l, 0], ln_b[l, 0],
                          seg_rows=batch * n_ctx, row0=batch)
        moe_args = (moe_router[l], w1_b[l], w3_b[l], w2_b[l], ln_g[l, 1], ln_b[l, 1])
        xl = _moe(xl, modrows, *moe_args, seg_rows=seq, row0=0, row_stride=1)
        if keep_ctx:
            xc = _moe(xc, modrows, *moe_args, seg_rows=n_ctx, row0=batch, row_stride=0)
    return xl.reshape(batch, seq, d)
```

```python
import functools

import numpy as np
import jax
import jax.numpy as jnp
from jax import lax
from jax.experimental import pallas as pl
from jax.experimental.pallas import tpu as pltpu

F32 = jnp.float32
BF16 = jnp.bfloat16
I32 = jnp.int32

D_MODEL = 2048
DEPTH = 4
GRID_W = 64
HEAD_DIM = 128
N_GROUPS = 4
GROUP_WIDTH = D_MODEL // N_GROUPS
GROUP_HEADS = GROUP_WIDTH // HEAD_DIM
GDN_CONV = 5
GDN_CHUNK = 64
ROWS = 256
GLA_DK = HEAD_DIM // 2
GLA_RANK = 16
GLA_TAU = 16.0
NA_ROWS = 8
NA_COLS = 16
ROPE_BASE = 10000.0
N_EXPERTS = 16
EC_CAPACITY = 2
D_EXPERT = D_MODEL // 2
DEEPNORM_ALPHA = (2 * DEPTH) ** 0.25
LN_EPS = 1e-5
NORM_EPS = 1e-6
NEG_BIG = -1e30

LANES = 128
SUBLANES = 8
VMEM_LIMIT = 56 * 1024 * 1024

_OFF = {}
_o = 0
for _name, _w in (("a_q", 512), ("a_k", 512), ("a_v", 512), ("a_z", 512), ("a_beta", 8), ("a_dec", 8),
                  ("b_q", 256), ("b_k", 256), ("b_v", 512), ("b_r", 512), ("b_code", 32),
                  ("c_q", 512), ("c_i", 512), ("c_g", 512), ("c_f", 1024),
                  ("d_q", 512), ("d_k", 512), ("d_v", 512)):
    _OFF[_name] = _o
    _o += _w
N_IN = _o

BLK = dict(a_q=0, a_k=4, a_v=8, a_z=12, c_q=16, c_i=20, c_g=24, c_f0=28, c_f1=32,
           d_q=36, d_k=40, d_v=44, b_v=48, b_r=52, b_q=56, b_k=60, small=64)
NP_BLOCKS = 65
NP = NP_BLOCKS * LANES
SM_BETA, SM_DEC, SM_CODE = 0, 8, 16


def _regroup_w_in(w_in, dtype=BF16):
    def cols(a, b):
        return w_in[..., a:b]

    def heads_padded(name):
        w = cols(_OFF[name], _OFF[name] + GROUP_HEADS * GLA_DK)
        w = w.reshape(w.shape[:-1] + (GROUP_HEADS, GLA_DK))
        w = jnp.pad(w, [(0, 0)] * (w.ndim - 1) + [(0, LANES - GLA_DK)])
        return w.reshape(w.shape[:-2] + (GROUP_HEADS * LANES,))

    small = jnp.concatenate([cols(_OFF["a_beta"], _OFF["a_beta"] + 16), cols(_OFF["b_code"], _OFF["b_code"] + 32)],
                            axis=-1)
    small = jnp.pad(small, [(0, 0)] * (small.ndim - 1) + [(0, LANES - 48)])
    parts = [cols(_OFF["a_q"], _OFF["a_q"] + 2048),
             cols(_OFF["c_q"], _OFF["c_q"] + 2560),
             cols(_OFF["d_q"], _OFF["d_q"] + 1536),
             cols(_OFF["b_v"], _OFF["b_v"] + 1024),
             heads_padded("b_q"), heads_padded("b_k"), small]
    return jnp.concatenate(parts, axis=-1).astype(dtype)


def _cparams(sem):
    return pltpu.CompilerParams(dimension_semantics=sem, vmem_limit_bytes=VMEM_LIMIT)


def _sigmoid(x):
    return 1.0 / (1.0 + jnp.exp(-x))


def _silu(x):
    return x * _sigmoid(x)


def _split2(x):
    hi = x.astype(BF16)
    lo = (x - hi.astype(F32)).astype(BF16)
    return hi, lo


def _split3(x):
    hi = x.astype(BF16)
    r = x - hi.astype(F32)
    mid = r.astype(BF16)
    lo = (r - mid.astype(F32)).astype(BF16)
    return hi, mid, lo


def _mm(a, b):
    return jnp.dot(a, b, preferred_element_type=F32)


def _mm_nt(a, b):
    return lax.dot_general(a, b, (((1,), (1,)), ((), ())), preferred_element_type=F32)


def _mm_tn(a, b):
    return lax.dot_general(a, b, (((0,), (0,)), ((), ())), preferred_element_type=F32)


def _mm_x2(a, b):
    ah, al = _split2(a)
    bh, bl = _split2(b)
    return _mm(ah, bh) + _mm(ah, bl) + _mm(al, bh)


def _mm_exact_lhs(a_bf, b):
    bh, bm, bl = _split3(b)
    return _mm(a_bf, bh) + _mm(a_bf, bm) + _mm(a_bf, bl)


def _mm_exact_rhs(a, b_bf):
    ah, am, al = _split3(a)
    return _mm(ah, b_bf) + _mm(am, b_bf) + _mm(al, b_bf)


def _layer_stats(x):
    mu = jnp.mean(x, axis=-1, keepdims=True)
    xc = x - mu
    var = jnp.mean(xc * xc, axis=-1, keepdims=True)
    return xc * lax.rsqrt(var + LN_EPS)


def _mod_kernel(c_ref, w_ref, b_ref, o_ref):
    s = _silu(c_ref[...])
    hi, lo = _split2(s)
    w = w_ref[0].astype(BF16)
    o_ref[0] = _mm(hi, w) + _mm(lo, w) + b_ref[0]


def _modulation(cc, w_mod, b_mod):
    depth, d, n6 = w_mod.shape
    tn = 1024
    return pl.pallas_call(
        _mod_kernel,
        grid=(depth, n6 // tn),
        in_specs=[pl.BlockSpec((8, d), lambda l, j: (0, 0)),
                  pl.BlockSpec((1, d, tn), lambda l, j: (l, 0, j)),
                  pl.BlockSpec((1, 1, tn), lambda l, j: (l, 0, j))],
        out_specs=pl.BlockSpec((1, 8, tn), lambda l, j: (l, 0, j)),
        out_shape=jax.ShapeDtypeStruct((depth, 8, n6), F32),
        compiler_params=_cparams(("arbitrary", "arbitrary")),
        name="modulation",
    )(cc, w_mod, b_mod.reshape(depth, 1, n6))


def _inproj_kernel(x_ref, sh_ref, sc_ref, w_ref, o_ref, u_scr):
    @pl.when(pl.program_id(1) == 0)
    def _():
        y = _layer_stats(x_ref[...]) * (1.0 + sc_ref[0]) + sh_ref[0]
        u_scr[...] = y.astype(BF16)

    o_ref[...] = _mm(u_scr[...], w_ref[0])


def _inproj(x, modrows, w, layer, *, seg_rows, row0):
    n, d = x.shape
    tm = min(512, n, seg_rows)
    tn = 13 * LANES
    tiles_per_seg = seg_rows // tm

    def mrow(k):
        return lambda i, j: ((row0 + i // tiles_per_seg) * 6 + k, 0, 0)

    return pl.pallas_call(
        _inproj_kernel,
        grid=(n // tm, NP // tn),
        in_specs=[pl.BlockSpec((tm, d), lambda i, j: (i, 0)),
                  pl.BlockSpec((1, 1, d), mrow(0)),
                  pl.BlockSpec((1, 1, d), mrow(1)),
                  pl.BlockSpec((1, d, tn), lambda i, j: (layer, 0, j))],
        out_specs=pl.BlockSpec((tm, tn), lambda i, j: (i, j)),
        out_shape=jax.ShapeDtypeStruct((n, NP), F32),
        scratch_shapes=[pltpu.VMEM((tm, d), BF16)],
        compiler_params=_cparams(("arbitrary", "arbitrary")),
        name="inproj",
    )(x, modrows, modrows, w)


def _outproj_kernel(m0, m1, m2, m3, w_ref, x_ref, gate_ref, g_ref, b_ref, o_ref):
    gw = GROUP_WIDTH
    acc = _mm(m0[...], w_ref[0, 0 * gw:1 * gw, :])
    acc += _mm(m1[...], w_ref[0, 1 * gw:2 * gw, :])
    acc += _mm(m2[...], w_ref[0, 2 * gw:3 * gw, :])
    acc += _mm(m3[...], w_ref[0, 3 * gw:4 * gw, :])
    y = DEEPNORM_ALPHA * x_ref[...] + gate_ref[0] * acc
    o_ref[...] = _layer_stats(y) * g_ref[...] + b_ref[...]


def _outproj(mixes, w, layer, x, modrows, g, b, *, seg_rows, row0):
    n, d = x.shape
    tm = min(512, n, seg_rows)
    tiles_per_seg = seg_rows // tm
    mspec = pl.BlockSpec((tm, GROUP_WIDTH), lambda i: (i, 0))
    return pl.pallas_call(
        _outproj_kernel,
        grid=(n // tm,),
        in_specs=[mspec, mspec, mspec, mspec,
                  pl.BlockSpec((1, d, d), lambda i: (layer, 0, 0)),
                  pl.BlockSpec((tm, d), lambda i: (i, 0)),
                  pl.BlockSpec((1, 1, d), lambda i: ((row0 + i // tiles_per_seg) * 6 + 2, 0, 0)),
                  pl.BlockSpec((1, d), lambda i: (0, 0)),
                  pl.BlockSpec((1, d), lambda i: (0, 0))],
        out_specs=pl.BlockSpec((tm, d), lambda i: (i, 0)),
        out_shape=jax.ShapeDtypeStruct((n, d), F32),
        compiler_params=_cparams(("arbitrary",)),
        name="outproj",
    )(*mixes, w, x, modrows, g.reshape(1, d), b.reshape(1, d))


def _bwd_chunk_index(i, n_ctx_chunks, n_chunks):
    return jnp.where(i < n_ctx_chunks, n_ctx_chunks - 1 - i, n_chunks + n_ctx_chunks - 1 - i)


def _rms_gate_store(of_s, ob_s, g_ref, gate_fn, gate_refs, out_refs, seg_bounds):
    for (lo, hi), gate_ref, out_ref in zip(seg_bounds, gate_refs, out_refs):
        n = hi - lo
        tile = min(512, n)
        for r0 in range(0, n, tile):
            o = of_s[lo + r0: lo + r0 + tile, :] + ob_s[lo + r0: lo + r0 + tile, :]
            y = o * lax.rsqrt(jnp.mean(o * o, axis=-1, keepdims=True) + NORM_EPS) * g_ref[...]
            out_ref[r0:r0 + tile, :] = (y * gate_fn(gate_ref[r0:r0 + tile, :])).astype(out_ref.dtype)


def _select_col(x, lane, c):
    return jnp.sum(jnp.where(lane == c, x, 0.0), axis=1, keepdims=True)


def _rows8(row):
    return jnp.broadcast_to(row, (SUBLANES, LANES))


def _gdn_kernel(ql, kl, vl, zl, sl, qc, kc, vc, zc, sc, wq, wk, wv, alog, dtb, gn,
                ol, oc, q_s, k_s, v_s, tok_s, pad_s, u_s, w_s, qg_s, kd_s, p_s, ge_s, *, n_ctx, n_lat):
    h = pl.program_id(1)
    n_tot = n_ctx + n_lat
    segs = ((0, n_ctx), (n_ctx, n_tot))

    def conv_into(x_ref, w_ref, dst, lo, n, l2_scale):
        pad_s[0:8, :] = jnp.zeros((8, LANES), F32)
        pad_s[8:8 + n, :] = x_ref[...]
        pad_s[8 + n:16 + n, :] = jnp.zeros((8, LANES), F32)
        tile = min(512, n)
        for r0 in range(0, n, tile):
            acc = jnp.zeros((tile, LANES), F32)
            for i in range(GDN_CONV):
                s0 = 8 + r0 + i - GDN_CONV // 2
                acc = acc + pad_s[s0:s0 + tile, :] * w_ref[i:i + 1, :]
            y = _silu(acc)
            if l2_scale is not None:
                y = y * (lax.rsqrt(jnp.sum(y * y, axis=-1, keepdims=True) + NORM_EPS) * l2_scale)
            dst[lo + r0: lo + r0 + tile, :] = y

    for (lo, hi), (xq, xk, xv) in zip(segs, ((qc, kc, vc), (ql, kl, vl))):
        conv_into(xq, wq, q_s, lo, hi - lo, HEAD_DIM ** -0.5)
        conv_into(xk, wk, k_s, lo, hi - lo, 1.0)
        conv_into(xv, wv, v_s, lo, hi - lo, None)

    def tok_into(s_ref, lo, n):
        tile = min(512, n)
        lane = lax.broadcasted_iota(I32, (tile, LANES), 1)
        for r0 in range(0, n, tile):
            sm = s_ref[r0:r0 + tile, :]
            cols = []
            for d in range(2):
                beta = _sigmoid(_select_col(sm, lane, SM_BETA + d * GROUP_HEADS + h))
                dec = _select_col(sm, lane, SM_DEC + d * GROUP_HEADS + h)
                a_neg = -jnp.exp(_select_col(alog[...], lane[0:1], d * GROUP_HEADS + h))
                bias = _select_col(dtb[...], lane[0:1], d * GROUP_HEADS + h)
                la = a_neg * jax.nn.softplus(dec + bias)
                cols += [beta, la]
            t = jnp.where(lane == 0, cols[0], jnp.where(lane == 1, cols[1],
                          jnp.where(lane == 2, cols[2], jnp.where(lane == 3, cols[3], 0.0))))
            tok_s[lo + r0: lo + r0 + tile, :] = t

    tok_into(sc, 0, n_ctx)
    tok_into(sl, n_ctx, n_lat)

    c = GDN_CHUNK
    nb = ROWS // c
    r = lax.broadcasted_iota(I32, (ROWS, ROWS), 0)
    cc = lax.broadcasted_iota(I32, (ROWS, ROWS), 1)
    same = (r // c) == (cc // c)
    same_bf = same.astype(BF16)
    eye = r == cc
    eye_f = eye.astype(F32)

    def inv_unit(a):
        t = eye_f - a
        pb = a.astype(BF16)
        for _ in range(5):
            pb = _mm(pb, pb).astype(BF16)
            t = t + _mm(t.astype(BF16), pb)
        return t

    def phase1(si, _):
        off = pl.multiple_of(si * ROWS, ROWS)
        q = q_s[pl.ds(off, ROWS), :]
        k = k_s[pl.ds(off, ROWS), :]
        v = v_s[pl.ds(off, ROWS), :]
        tk = tok_s[pl.ds(off, ROWS), :]
        qb = q.astype(BF16)
        kb = k.astype(BF16)
        kk = _mm_nt(kb, kb)
        qk = _mm_nt(qb, kb)
        for d, rev in ((0, False), (1, True)):
            beta = tk[:, 2 * d:2 * d + 1]
            la = tk[:, 2 * d + 1:2 * d + 2]
            incl = same & ((cc >= r) if rev else (cc <= r))
            strict = same & ((cc > r) if rev else (cc < r))
            gi_full = _mm_exact_lhs(incl.astype(BF16), jnp.broadcast_to(la, (ROWS, LANES)))
            gi = jnp.concatenate([gi_full, gi_full], axis=1)
            gj = _mm_exact_lhs(same_bf, jnp.where(eye, gi, 0.0))
            diff = gi - gj
            a = beta * kk * jnp.exp(jnp.where(strict, diff, -jnp.inf))
            p = qk * jnp.exp(jnp.where(incl, diff, -jnp.inf))
            for kb_ in range(nb):
                p_s[d, pl.ds(pl.multiple_of(off + kb_ * c, c), c), :] = (
                    p[kb_ * c:(kb_ + 1) * c, kb_ * c:(kb_ + 1) * c].astype(BF16))
            eg = jnp.exp(gi_full)
            rhs = jnp.concatenate([beta * v, (beta * eg) * k], axis=1)
            sol = _mm_x2(inv_unit(a), rhs)
            u_s[d, pl.ds(off, ROWS), :] = sol[:, :LANES]
            w_s[d, pl.ds(off, ROWS), :] = sol[:, LANES:].astype(BF16)
            gr = gi_full.reshape(nb, c, LANES)
            gl = gr[:, 0:1, :] if rev else gr[:, c - 1:c, :]
            g_last = jnp.broadcast_to(gl, (nb, c, LANES)).reshape(ROWS, LANES)
            qg_s[d, pl.ds(off, ROWS), :] = (q * eg).astype(BF16)
            kd_s[d, pl.ds(off, ROWS), :] = (k * jnp.exp(g_last - gi_full)).astype(BF16)
            ge = jnp.broadcast_to(jnp.exp(gl), (nb, SUBLANES, LANES)).reshape(nb * SUBLANES, LANES)
            ge_s[d, pl.ds(pl.multiple_of(si * nb * SUBLANES, nb * SUBLANES), nb * SUBLANES), :] = ge
        return 0

    lax.fori_loop(0, n_tot // ROWS, phase1, 0)

    of_s, ob_s = q_s, k_s
    n_cc = n_ctx // c
    n_ch = n_tot // c

    def step(i, carry):
        new = []
        for d, (st, ch) in enumerate(zip(carry, (i, _bwd_chunk_index(i, n_cc, n_ch)))):
            off = pl.multiple_of(ch * c, c)
            w = w_s[d, pl.ds(off, c), :]
            qg = qg_s[d, pl.ds(off, c), :]
            ge = ge_s[d, pl.ds(pl.multiple_of(ch * SUBLANES, SUBLANES), SUBLANES), :][0:1, :]
            stb = st.astype(BF16)
            r1 = _mm(jnp.concatenate([w, qg], axis=0), stb)
            vb = (u_s[d, pl.ds(off, c), :] - r1[:c]).astype(BF16)
            o = r1[c:] + _mm(p_s[d, pl.ds(off, c), :], vb)
            new.append(ge * st + _mm_tn(kd_s[d, pl.ds(off, c), :], vb))
            (ob_s if d else of_s)[pl.ds(off, c), :] = o
        return tuple(new)

    z0 = jnp.zeros((HEAD_DIM, HEAD_DIM), F32)
    lax.fori_loop(0, n_ch, step, (z0, z0))

    _rms_gate_store(of_s, ob_s, gn, _silu, (zc, zl), (oc, ol), segs)


def _gdn(p_lat, p_ctx, conv_w, a_log, dt_bias, gnorm, *, batch):
    n_lat = p_lat.shape[0] // batch
    n_ctx = p_ctx.shape[0] // batch
    n_tot = n_lat + n_ctx
    assert n_lat % ROWS == 0 and n_ctx % ROWS == 0
    n_ch = n_tot // GDN_CHUNK

    def blk(n, base):
        return pl.BlockSpec((n, LANES), lambda b, h: (b, base + h))

    def small(n):
        return pl.BlockSpec((n, LANES), lambda b, h: (b, BLK["small"]))

    def cw(base):
        return pl.BlockSpec((8, LANES), lambda b, h: (0, base + h))

    row = pl.BlockSpec((1, LANES), lambda b, h: (0, 0))
    conv_p = jnp.zeros((8, 3 * GROUP_WIDTH), F32).at[:GDN_CONV].set(conv_w)
    alog_row = jnp.zeros((1, LANES), F32).at[0, :2 * GROUP_HEADS].set(a_log.reshape(-1))
    dtb_row = jnp.zeros((1, LANES), F32).at[0, :2 * GROUP_HEADS].set(dt_bias.reshape(-1))
    lat_in = [blk(n_lat, BLK[k]) for k in ("a_q", "a_k", "a_v", "a_z")] + [small(n_lat)]
    ctx_in = [blk(n_ctx, BLK[k]) for k in ("a_q", "a_k", "a_v", "a_z")] + [small(n_ctx)]
    kern = functools.partial(_gdn_kernel, n_ctx=n_ctx, n_lat=n_lat)
    return pl.pallas_call(
        kern,
        grid=(batch, GROUP_HEADS),
        in_specs=lat_in + ctx_in + [cw(0), cw(4), cw(8), row, row, row],
        out_specs=[pl.BlockSpec((n_lat, LANES), lambda b, h: (b, h)),
                   pl.BlockSpec((n_ctx, LANES), lambda b, h: (b, h))],
        out_shape=[jax.ShapeDtypeStruct((batch * n_lat, GROUP_WIDTH), BF16),
                   jax.ShapeDtypeStruct((batch * n_ctx, GROUP_WIDTH), BF16)],
        scratch_shapes=[pltpu.VMEM((n_tot, LANES), F32)] * 4
        + [pltpu.VMEM((n_lat + 16, LANES), F32),
           pltpu.VMEM((2, n_tot, LANES), F32),
           pltpu.VMEM((2, n_tot, LANES), BF16),
           pltpu.VMEM((2, n_tot, LANES), BF16),
           pltpu.VMEM((2, n_tot, LANES), BF16),
           pltpu.VMEM((2, n_tot, GDN_CHUNK), BF16),
           pltpu.VMEM((2, n_ch * SUBLANES, LANES), F32)],
        compiler_params=_cparams(("arbitrary", "arbitrary")),
        name="gdn",
    )(*([p_lat] * 5), *([p_ctx] * 5), conv_p, conv_p, conv_p, alog_row, dtb_row, gnorm.reshape(1, LANES))


DROWS = 128
N_LEVELS = 7


def _cumsum_matrices():
    i = np.arange(DROWS)[:, None]
    u = np.arange(DROWS)[None, :]
    return np.stack([(u <= i), (u >= i)]).astype(np.float32)


_LOW_MATS = _cumsum_matrices()


def _level_index(rev):
    r = lax.broadcasted_iota(I32, (DROWS, DROWS), 0)
    cc = lax.broadcasted_iota(I32, (DROWS, DROWS), 1)
    lev = 31 - lax.clz(r ^ cc)
    earlier = (cc > r) if rev else (cc < r)
    return jnp.where(earlier, lev, jnp.where(r == cc, -1, -2))


def _rows_at(b, rowi, m, shifts, cache):
    c = b.shape[0]
    out = jnp.zeros_like(b)
    for mval, sh in enumerate(shifts):
        if sh not in cache:
            cache[sh] = b if sh == 0 else pltpu.roll(b, sh % c, 0)
        src = rowi - sh
        cand = jnp.where((src >= 0) & (src < c), cache[sh], 0.0)
        out = jnp.where(m == mval, cand, out)
    return out


def _diag_phase1(q, k, v_bf, la, rev, low_ref, lev):
    c = DROWS
    b = _mm_exact_lhs(low_ref[1 if rev else 0], la)
    rowi = lax.broadcasted_iota(I32, (c, LANES), 0)
    win = {0: la}
    wout = {}
    rolled = {}
    for lv in (1, 2):
        s = 1 << lv
        m = rowi & (s - 1)
        if rev:
            win[lv] = b - _rows_at(b, rowi, m, [-(s - t) for t in range(s)], rolled)
            wout[lv] = _rows_at(b, rowi, m, list(range(s)), rolled) - b
        else:
            win[lv] = b - _rows_at(b, rowi, m, [t + 1 for t in range(s)], rolled)
            wout[lv] = _rows_at(b, rowi, m, [-(s - 1 - t) for t in range(s)], rolled) - b
    for lv in range(3, N_LEVELS + 1):
        s = 1 << lv
        nblk = c // s
        br = b.reshape(nblk, s, LANES)
        edge = br[:, 0:1, :] if rev else br[:, s - 1:s, :]
        zero = jnp.zeros((1, 1, LANES), F32)
        if nblk == 1:
            before = zero
        elif rev:
            before = jnp.concatenate([edge[1:], zero], axis=0)
        else:
            before = jnp.concatenate([zero, edge[:-1]], axis=0)
        win[lv] = (br - before).reshape(c, LANES)
        wout[lv] = (edge - br).reshape(c, LANES)
    qb = q.astype(BF16)
    kb = k.astype(BF16)
    p = jnp.where(lev == -1, _mm_nt(qb, kb), 0.0)
    for lv in range(N_LEVELS):
        qs = (q * jnp.exp(win[lv])).astype(BF16)
        ks = kb if lv == 0 else (k * jnp.exp(wout[lv])).astype(BF16)
        p = jnp.where(lev == lv, _mm_nt(qs, ks), p)
    o_intra = _mm(p.astype(BF16), v_bf)
    qg = (q * jnp.exp(win[N_LEVELS])).astype(BF16)
    kd = (k * jnp.exp(wout[N_LEVELS])).astype(BF16)
    a_end = jnp.exp(b[0:1, :] if rev else b[c - 1:c, :])
    return o_intra, qg, kd, a_end


def _diag_store_phase1(vals, off, ci, low_ref, levs, v_bs, o_s, qg_s, kd_s, ae_s):
    q, kf, kb, v, laf, lab = vals
    v_bf = v.astype(BF16)
    v_bs[pl.ds(off, DROWS), :] = v_bf
    for d, (k, la) in enumerate(((kf, laf), (kb, lab))):
        o_intra, qg, kd, a_end = _diag_phase1(q, k, v_bf, la, bool(d), low_ref, levs[d])
        o_s[d][pl.ds(off, DROWS), :] = o_intra
        qg_s[d, pl.ds(off, DROWS), :] = qg
        kd_s[d, pl.ds(off, DROWS), :] = kd
        ae_s[d, pl.ds(pl.multiple_of(ci * SUBLANES, SUBLANES), SUBLANES), :] = _rows8(a_end)


def _diag_phase2(v_bs, o_s, qg_s, kd_s, ae_s, n_ctx, n_tot):
    n_cc = n_ctx // DROWS
    n_ch = n_tot // DROWS

    def step(i, carry):
        new = []
        for d, (st, ch) in enumerate(zip(carry, (i, _bwd_chunk_index(i, n_cc, n_ch)))):
            off = pl.multiple_of(ch * DROWS, DROWS)
            a_end = ae_s[d, pl.ds(pl.multiple_of(ch * SUBLANES, SUBLANES), SUBLANES), :][0:1, :]
            o_s[d][pl.ds(off, DROWS), :] += _mm_nt(qg_s[d, pl.ds(off, DROWS), :], st.astype(BF16))
            new.append(st * a_end + _mm_tn(v_bs[pl.ds(off, DROWS), :], kd_s[d, pl.ds(off, DROWS), :]))
        return tuple(new)

    z0 = jnp.zeros((HEAD_DIM, HEAD_DIM), F32)
    lax.fori_loop(0, n_ch, step, (z0, z0))


def _diag_scratch(n_tot):
    n_ch = n_tot // DROWS
    return [pltpu.VMEM((n_tot, LANES), BF16),
            pltpu.VMEM((n_tot, LANES), F32),
            pltpu.VMEM((n_tot, LANES), F32),
            pltpu.VMEM((2, n_tot, LANES), BF16),
            pltpu.VMEM((2, n_tot, LANES), BF16),
            pltpu.VMEM((2, n_ch * SUBLANES, LANES), F32)]


def _gla_kernel(ql, kl, vl, rl, sl, qc, kc, vc, rc, sc, cos_ref, sin_ref, perm_ref, wg_ref, bg_ref, gn,
                low_ref, ol, oc, v_bs, of_s, ob_s, qg_s, kd_s, ae_s, *, n_ctx, n_lat):
    n_tot = n_ctx + n_lat
    segs = ((0, n_ctx), (n_ctx, n_tot))
    levs = (_level_index(False), _level_index(True))

    for (lo, hi), (xq, xk, xv, xs), rope in zip(segs, ((qc, kc, vc, sc), (ql, kl, vl, sl)), (False, True)):
        def chunk(ci, _, lo=lo, xq=xq, xk=xk, xv=xv, xs=xs, rope=rope):
            r0 = pl.multiple_of(ci * DROWS, DROWS)
            q = xq[pl.ds(r0, DROWS), :]
            k = xk[pl.ds(r0, DROWS), :]
            if rope:
                cs = cos_ref[pl.ds(r0, DROWS), :]
                sn = sin_ref[pl.ds(r0, DROWS), :]
                q = q * cs + _mm_exact_rhs(q, perm_ref[...]) * sn
                k = k * cs + _mm_exact_rhs(k, perm_ref[...]) * sn
            sm = xs[pl.ds(r0, DROWS), :]
            las = [jax.nn.log_sigmoid(_mm_x2(sm, wg_ref[d, 0]) + bg_ref[d, 0]) * (1.0 / GLA_TAU) for d in range(2)]
            vals = (q * (GLA_DK ** -0.5), k, k, xv[pl.ds(r0, DROWS), :], las[0], las[1])
            _diag_store_phase1(vals, pl.multiple_of(lo + r0, DROWS), lo // DROWS + ci, low_ref, levs,
                               v_bs, (of_s, ob_s), qg_s, kd_s, ae_s)
            return 0

        lax.fori_loop(0, (hi - lo) // DROWS, chunk, 0, unroll=2)

    _diag_phase2(v_bs, (of_s, ob_s), qg_s, kd_s, ae_s, n_ctx, n_tot)
    _rms_gate_store(of_s, ob_s, gn, _silu, (rc, rl), (oc, ol), segs)


def _rope_tables(n_lat):
    t = np.arange(n_lat)
    nf = GLA_DK // 4
    inv = (ROPE_BASE ** (-jnp.arange(nf, dtype=F32) / nf))
    lane = np.arange(LANES)
    f = lane % nf
    use_col = (lane % GLA_DK) >= GLA_DK // 2
    first = (lane % (2 * nf)) < nf
    real = lane < GLA_DK
    pos = jnp.where(use_col[None, :], (t % GRID_W)[:, None], (t // GRID_W)[:, None]).astype(F32)
    ang = pos * inv[f][None, :]
    cos = jnp.where(real[None, :], jnp.cos(ang), 1.0)
    sin = jnp.where(real[None, :], jnp.where(first[None, :], -jnp.sin(ang), jnp.sin(ang)), 0.0)
    partner = np.where(first, lane + nf, lane - nf)
    perm = np.zeros((LANES, LANES), np.float32)
    perm[partner[real], lane[real]] = 1.0
    return cos.astype(F32), sin.astype(F32), jnp.asarray(perm, BF16)


def _gla(p_lat, p_ctx, w_gate, b_gate, gnorm, low, *, batch):
    n_lat = p_lat.shape[0] // batch
    n_ctx = p_ctx.shape[0] // batch
    n_tot = n_lat + n_ctx
    assert n_lat % DROWS == 0 and n_ctx % DROWS == 0
    cos, sin, perm = _rope_tables(n_lat)
    wg = jnp.zeros((2, GROUP_HEADS, LANES, LANES), F32)
    wsrc = w_gate.reshape(2, GLA_RANK, GROUP_HEADS, GLA_DK).transpose(0, 2, 1, 3)
    for d in range(2):
        wg = wg.at[d, :, SM_CODE + d * GLA_RANK: SM_CODE + (d + 1) * GLA_RANK, :GLA_DK].set(wsrc[d])
    bg = jnp.zeros((2, GROUP_HEADS, 1, LANES), F32).at[:, :, 0, :GLA_DK].set(
        b_gate.reshape(2, GROUP_HEADS, GLA_DK))

    def blk(n, base):
        return pl.BlockSpec((n, LANES), lambda b, h: (b, base + h))

    def small(n):
        return pl.BlockSpec((n, LANES), lambda b, h: (b, BLK["small"]))

    def const2(shape):
        return pl.BlockSpec(shape, lambda b, h: (0, 0))

    names = ("b_q", "b_k", "b_v", "b_r")
    kern = functools.partial(_gla_kernel, n_ctx=n_ctx, n_lat=n_lat)
    return pl.pallas_call(
        kern,
        grid=(batch, GROUP_HEADS),
        in_specs=[blk(n_lat, BLK[k]) for k in names] + [small(n_lat)]
        + [blk(n_ctx, BLK[k]) for k in names] + [small(n_ctx)]
        + [const2((n_lat, LANES)), const2((n_lat, LANES)), const2((LANES, LANES)),
           pl.BlockSpec((2, 1, LANES, LANES), lambda b, h: (0, h, 0, 0)),
           pl.BlockSpec((2, 1, 1, LANES), lambda b, h: (0, h, 0, 0)),
           const2((1, LANES)),
           pl.BlockSpec(low.shape, lambda b, h: (0, 0, 0))],
        out_specs=[pl.BlockSpec((n_lat, LANES), lambda b, h: (b, h)),
                   pl.BlockSpec((n_ctx, LANES), lambda b, h: (b, h))],
        out_shape=[jax.ShapeDtypeStruct((batch * n_lat, GROUP_WIDTH), BF16),
                   jax.ShapeDtypeStruct((batch * n_ctx, GROUP_WIDTH), BF16)],
        scratch_shapes=_diag_scratch(n_tot),
        compiler_params=_cparams(("arbitrary", "arbitrary")),
        name="gla",
    )(*([p_lat] * 5), *([p_ctx] * 5), cos, sin, perm, wg, bg, gnorm.reshape(1, LANES), low)


def _hgrn_kernel(ql, il, gl, f0l, f1l, qc, ic, gc, f0c, f1c, gam_ref, gn, low_ref,
                 ol, oc, v_bs, of_s, ob_s, qg_s, kd_s, ae_s, *, n_ctx, n_lat, layer):
    n_tot = n_ctx + n_lat
    segs = ((0, n_ctx), (n_ctx, n_tot))
    levs = (_level_index(False), _level_index(True))
    lbs = []
    for d in range(2):
        gam = gam_ref[d]
        ex = jnp.exp(gam - jnp.max(gam, axis=0, keepdims=True))
        pr = ex / jnp.sum(ex, axis=0, keepdims=True)
        lb = jnp.zeros((1, LANES), F32)
        for m in range(1, layer + 1):
            lb = lb + pr[m:m + 1, :]
        lbs.append(lb)

    for (lo, hi), (xq, xi, xf0, xf1) in zip(segs, ((qc, ic, f0c, f1c), (ql, il, f0l, f1l))):
        def chunk(ci, _, lo=lo, xq=xq, xi=xi, xf0=xf0, xf1=xf1):
            r0 = pl.multiple_of(ci * DROWS, DROWS)
            fs = [lbs[d] + (1.0 - lbs[d]) * _sigmoid(xf[pl.ds(r0, DROWS), :]) for d, xf in enumerate((xf0, xf1))]
            vals = (xq[pl.ds(r0, DROWS), :], 1.0 - fs[0], 1.0 - fs[1], xi[pl.ds(r0, DROWS), :],
                    jnp.log(fs[0]), jnp.log(fs[1]))
            _diag_store_phase1(vals, pl.multiple_of(lo + r0, DROWS), lo // DROWS + ci, low_ref, levs,
                               v_bs, (of_s, ob_s), qg_s, kd_s, ae_s)
            return 0

        lax.fori_loop(0, (hi - lo) // DROWS, chunk, 0, unroll=2)

    _diag_phase2(v_bs, (of_s, ob_s), qg_s, kd_s, ae_s, n_ctx, n_tot)
    _rms_gate_store(of_s, ob_s, gn, _sigmoid, (gc, gl), (oc, ol), segs)


def _hgrn(p_lat, p_ctx, gamma, gnorm, low, *, batch, layer):
    n_lat = p_lat.shape[0] // batch
    n_ctx = p_ctx.shape[0] // batch
    n_tot = n_lat + n_ctx
    assert n_lat % DROWS == 0 and n_ctx % DROWS == 0

    def blk(n, base):
        return pl.BlockSpec((n, LANES), lambda b, h: (b, base + h))

    names = ("c_q", "c_i", "c_g", "c_f0", "c_f1")
    kern = functools.partial(_hgrn_kernel, n_ctx=n_ctx, n_lat=n_lat, layer=layer)
    return pl.pallas_call(
        kern,
        grid=(batch, GROUP_HEADS),
        in_specs=[blk(n_lat, BLK[k]) for k in names] + [blk(n_ctx, BLK[k]) for k in names]
        + [pl.BlockSpec((2, gamma.shape[1], LANES), lambda b, h: (0, 0, h)),
           pl.BlockSpec((1, LANES), lambda b, h: (0, 0)),
           pl.BlockSpec(low.shape, lambda b, h: (0, 0, 0))],
        out_specs=[pl.BlockSpec((n_lat, LANES), lambda b, h: (b, h)),
                   pl.BlockSpec((n_ctx, LANES), lambda b, h: (b, h))],
        out_shape=[jax.ShapeDtypeStruct((batch * n_lat, GROUP_WIDTH), BF16),
                   jax.ShapeDtypeStruct((batch * n_ctx, GROUP_WIDTH), BF16)],
        scratch_shapes=_diag_scratch(n_tot),
        compiler_params=_cparams(("arbitrary", "arbitrary")),
        name="hgrn",
    )(*([p_lat] * 5), *([p_ctx] * 5), gamma, gnorm.reshape(1, LANES), low)


NA_ROWS_PER_STEP = 2


def _na_kernel(ql, kl, vl, qc, kc, vc, bias_ref, ol, oc, *, n_rows):
    scale = HEAD_DIM ** -0.5
    kcb = kc[...].astype(BF16)
    vcb = vc[...].astype(BF16)
    w = GRID_W
    kr = min(NA_ROWS, n_rows)

    def one_row(rr):
        rs = jnp.clip(rr - kr // 2, 0, n_rows - kr)
        q = ql[pl.ds(pl.multiple_of(rr * w, w), w), :].astype(BF16)
        koff = pl.multiple_of(rs * w, w)
        kb = kl[pl.ds(koff, kr * w), :].astype(BF16)
        vb = vl[pl.ds(koff, kr * w), :].astype(BF16)
        s_loc = _mm_nt(q, kb) * scale + bias_ref[0, rr - rs]
        s_ctx = _mm_nt(q, kcb) * scale
        m = jnp.maximum(jnp.max(s_loc, axis=-1, keepdims=True), jnp.max(s_ctx, axis=-1, keepdims=True))
        p_loc = jnp.exp(s_loc - m)
        p_ctx = jnp.exp(s_ctx - m)
        den = jnp.sum(p_loc, axis=-1, keepdims=True) + jnp.sum(p_ctx, axis=-1, keepdims=True)
        o = _mm(p_loc.astype(BF16), vb) + _mm(p_ctx.astype(BF16), vcb)
        ol[pl.ds(pl.multiple_of(rr * w, w), w), :] = (o / den).astype(ol.dtype)

    def rows(it, _):
        for j in range(NA_ROWS_PER_STEP):
            one_row(it * NA_ROWS_PER_STEP + j)
        return 0

    lax.fori_loop(0, n_rows // NA_ROWS_PER_STEP, rows, 0)

    s = _mm_nt(qc[...].astype(BF16), kcb) * scale
    p = jnp.exp(s - jnp.max(s, axis=-1, keepdims=True))
    o = _mm(p.astype(BF16), vcb) / jnp.sum(p, axis=-1, keepdims=True)
    oc[...] = o.astype(oc.dtype)


def _na_bias(rpb, n_rows):
    kr = min(NA_ROWS, n_rows)
    off = np.arange(kr)
    j = np.arange(kr)
    cq = np.arange(GRID_W)
    dr = j[None, :] - off[:, None] + NA_ROWS - 1
    dr_ok = (dr >= 0) & (dr < 2 * NA_ROWS - 1)
    dc = np.clip(cq[None, :] - cq[:, None], -(NA_COLS - 1), NA_COLS - 1) + NA_COLS - 1
    col_start = np.clip(cq - NA_COLS // 2, 0, GRID_W - NA_COLS)
    col_in = (cq[None, :] >= col_start[:, None]) & (cq[None, :] < col_start[:, None] + NA_COLS)
    b = rpb.astype(F32)[:, np.clip(dr, 0, 2 * NA_ROWS - 2)[:, None, :, None], dc[None, :, None, :]]
    ok = dr_ok[:, None, :, None] & col_in[None, :, None, :]
    b = jnp.where(ok[None], b, NEG_BIG)
    return b.reshape(rpb.shape[0], kr, GRID_W, kr * GRID_W)


def _na(p_lat, p_ctx, rpb, *, batch):
    n_lat = p_lat.shape[0] // batch
    n_ctx = p_ctx.shape[0] // batch
    n_rows = n_lat // GRID_W
    assert n_rows % NA_ROWS_PER_STEP == 0
    bias = _na_bias(rpb, n_rows)

    def blk(n, base):
        return pl.BlockSpec((n, LANES), lambda b, h: (b, base + h))

    names = ("d_q", "d_k", "d_v")
    return pl.pallas_call(
        functools.partial(_na_kernel, n_rows=n_rows),
        grid=(batch, GROUP_HEADS),
        in_specs=[blk(n_lat, BLK[k]) for k in names] + [blk(n_ctx, BLK[k]) for k in names]
        + [pl.BlockSpec((1,) + bias.shape[1:], lambda b, h: (h, 0, 0, 0))],
        out_specs=[pl.BlockSpec((n_lat, LANES), lambda b, h: (b, h)),
                   pl.BlockSpec((n_ctx, LANES), lambda b, h: (b, h))],
        out_shape=[jax.ShapeDtypeStruct((batch * n_lat, GROUP_WIDTH), BF16),
                   jax.ShapeDtypeStruct((batch * n_ctx, GROUP_WIDTH), BF16)],
        compiler_params=_cparams(("arbitrary", "arbitrary")),
        name="na",
    )(*([p_lat] * 3), *([p_ctx] * 3), bias)


def _router_kernel(x_ref, sh_ref, sc_ref, wr_ref, u_ref, code_ref, gate_ref, lg_scr, *, cap, n_tiles, tile):
    j = pl.program_id(1)
    u = _layer_stats(x_ref[...]) * (1.0 + sc_ref[0]) + sh_ref[0]
    u_ref[...] = u.astype(BF16)
    lg_scr[j] = lax.dot_general(wr_ref[...], u, (((1,), (1,)), ((), ())),
                                precision=lax.Precision.HIGHEST, preferred_element_type=F32)

    @pl.when(j == n_tiles - 1)
    def _():
        lg = lg_scr[...]
        ex = jnp.exp(lg - jnp.max(lg, axis=1, keepdims=True))
        aff = ex / jnp.sum(ex, axis=1, keepdims=True)
        bits = lax.bitcast_convert_type(aff, I32)

        def count(mask):
            per = jnp.sum(mask.astype(F32), axis=0)
            return jnp.sum(per, axis=1, keepdims=True)

        def bis(it, thr):
            cand = thr | jnp.left_shift(jnp.int32(1), 30 - it)
            ok = count(bits >= cand[None]) >= float(cap)
            return jnp.where(ok, cand, thr)

        thr = lax.fori_loop(0, 31, bis, jnp.zeros((N_EXPERTS, 1), I32))
        gt = bits > thr[None]
        eq = bits == thr[None]
        need = float(cap) - count(gt)
        r = lax.broadcasted_iota(I32, (tile, tile), 0)
        c = lax.broadcasted_iota(I32, (tile, tile), 1)
        upper = (r < c).astype(BF16)
        carry_eq = jnp.zeros((N_EXPERTS, 1), F32)
        carry_sel = jnp.zeros((N_EXPERTS, 1), F32)
        for t in range(n_tiles):
            eq_t = eq[t].astype(BF16)
            pre_eq = _mm(eq_t, upper) + carry_eq
            sel = gt[t] | (eq[t] & (pre_eq < need))
            sel_b = sel.astype(BF16)
            pos = _mm(sel_b, upper) + carry_sel
            code_ref[0, t] = jnp.where(sel, pos.astype(I32), -1)
            gate_ref[0, t] = aff[t]
            carry_eq = carry_eq + jnp.sum(eq_t.astype(F32), axis=1, keepdims=True)
            carry_sel = carry_sel + jnp.sum(sel_b.astype(F32), axis=1, keepdims=True)


def _router(x, modrows, w_router, *, seg_rows, row0, row_stride):
    n, d = x.shape
    nseg = n // seg_rows
    tile = min(512, seg_rows)
    n_tiles = seg_rows // tile
    cap = EC_CAPACITY * seg_rows // N_EXPERTS
    kern = functools.partial(_router_kernel, cap=cap, n_tiles=n_tiles, tile=tile)
    return pl.pallas_call(
        kern,
        grid=(nseg, n_tiles),
        in_specs=[pl.BlockSpec((tile, d), lambda s, j: (s * n_tiles + j, 0)),
                  pl.BlockSpec((1, 1, d), lambda s, j: ((row0 + s * row_stride) * 6 + 3, 0, 0)),
                  pl.BlockSpec((1, 1, d), lambda s, j: ((row0 + s * row_stride) * 6 + 4, 0, 0)),
                  pl.BlockSpec((N_EXPERTS, d), lambda s, j: (0, 0))],
        out_specs=[pl.BlockSpec((tile, d), lambda s, j: (s * n_tiles + j, 0)),
                   pl.BlockSpec((1, n_tiles, N_EXPERTS, tile), lambda s, j: (s, 0, 0, 0)),
                   pl.BlockSpec((1, n_tiles, N_EXPERTS, tile), lambda s, j: (s, 0, 0, 0))],
        out_shape=[jax.ShapeDtypeStruct((n, d), BF16),
                   jax.ShapeDtypeStruct((nseg, n_tiles, N_EXPERTS, tile), I32),
                   jax.ShapeDtypeStruct((nseg, n_tiles, N_EXPERTS, tile), F32)],
        scratch_shapes=[pltpu.VMEM((n_tiles, N_EXPERTS, tile), F32)],
        compiler_params=_cparams(("arbitrary", "arbitrary")),
        name="router",
    )(x, modrows, modrows, w_router.T)


def _gather_kernel(u_ref, code_ref, o_ref, *, cap, n_tiles, tile):
    slot = lax.broadcasted_iota(I32, (cap, tile), 0)
    acc = jnp.zeros((cap, u_ref.shape[1]), F32)
    for t in range(n_tiles):
        oh = (code_ref[0, 0, t:t + 1, :] == slot).astype(BF16)
        acc = acc + _mm(oh, u_ref[t * tile:(t + 1) * tile, :])
    o_ref[0, 0] = acc.astype(BF16)


def _gather(u, code_e, *, seg_rows):
    n, d = u.shape
    nseg, _, n_tiles, tile = code_e.shape
    cap = EC_CAPACITY * seg_rows // N_EXPERTS
    return pl.pallas_call(
        functools.partial(_gather_kernel, cap=cap, n_tiles=n_tiles, tile=tile),
        grid=(nseg, N_EXPERTS),
        in_specs=[pl.BlockSpec((seg_rows, d), lambda s, e: (s, 0)),
                  pl.BlockSpec((1, 1, n_tiles, tile), lambda s, e: (s, e, 0, 0))],
        out_specs=pl.BlockSpec((1, 1, cap, d), lambda s, e: (e, s, 0, 0)),
        out_shape=jax.ShapeDtypeStruct((N_EXPERTS, nseg, cap, d), BF16),
        compiler_params=_cparams(("arbitrary", "arbitrary")),
        name="moe_gather",
    )(u, code_e)


def _ffn_kernel(x_ref, w1_ref, w3_ref, w2_ref, o_ref):
    x = x_ref[0]
    a = _mm(x, w1_ref[0, 0])
    g = _mm(x, w3_ref[0, 0])
    hid = (_silu(a) * g).astype(BF16)
    o_ref[0] = _mm(hid, w2_ref[0, 0]).astype(BF16)


def _ffn(xs, w1, w3, w2, layer):
    e, r, d = xs.shape
    f = w1.shape[3]
    tr = min(512, r)
    return pl.pallas_call(
        _ffn_kernel,
        grid=(e, r // tr),
        in_specs=[pl.BlockSpec((1, tr, d), lambda i, j: (i, j, 0)),
                  pl.BlockSpec((1, 1, d, f), lambda i, j: (layer, i, 0, 0)),
                  pl.BlockSpec((1, 1, d, f), lambda i, j: (layer, i, 0, 0)),
                  pl.BlockSpec((1, 1, f, d), lambda i, j: (layer, i, 0, 0))],
        out_specs=pl.BlockSpec((1, tr, d), lambda i, j: (i, j, 0)),
        out_shape=jax.ShapeDtypeStruct((e, r, d), BF16),
        compiler_params=_cparams(("arbitrary", "arbitrary")),
        name="moe_ffn",
    )(xs, w1, w3, w2)


def _combine_kernel(y_ref, code_ref, gate_ref, x_ref, m5_ref, g_ref, b_ref, o_ref, acc, *, cap):
    e = pl.program_id(2)

    @pl.when(e == 0)
    def _():
        acc[...] = jnp.zeros_like(acc)

    lane = lax.broadcasted_iota(I32, code_ref.shape[1:], 1)
    code = jnp.sum(jnp.where(lane == e, code_ref[0], 0), axis=1, keepdims=True)
    gate = jnp.sum(jnp.where(lane == e, gate_ref[0], 0.0), axis=1, keepdims=True)
    slot = lax.broadcasted_iota(I32, (code.shape[0], cap), 1)
    wm = jnp.where(code == slot, gate, 0.0).astype(BF16)
    acc[...] += _mm(wm, y_ref[0, 0])

    @pl.when(e == N_EXPERTS - 1)
    def _():
        y = DEEPNORM_ALPHA * x_ref[...] + m5_ref[0] * acc[...]
        o_ref[...] = _layer_stats(y) * g_ref[...] + b_ref[...]


def _combine(ys, code_t, gate_t, x, modrows, g, b, *, seg_rows, row0, row_stride):
    n, d = x.shape
    nseg = n // seg_rows
    cap = ys.shape[2]
    tj = min(512, seg_rows)
    nt = seg_rows // tj
    return pl.pallas_call(
        functools.partial(_combine_kernel, cap=cap),
        grid=(nseg, nt, N_EXPERTS),
        in_specs=[pl.BlockSpec((1, 1, cap, d), lambda s, j, e: (e, s, 0, 0)),
                  pl.BlockSpec((1, tj, N_EXPERTS), lambda s, j, e: (s, j, 0)),
                  pl.BlockSpec((1, tj, N_EXPERTS), lambda s, j, e: (s, j, 0)),
                  pl.BlockSpec((tj, d), lambda s, j, e: (s * nt + j, 0)),
                  pl.BlockSpec((1, 1, d), lambda s, j, e: ((row0 + s * row_stride) * 6 + 5, 0, 0)),
                  pl.BlockSpec((1, d), lambda s, j, e: (0, 0)),
                  pl.BlockSpec((1, d), lambda s, j, e: (0, 0))],
        out_specs=pl.BlockSpec((tj, d), lambda s, j, e: (s * nt + j, 0)),
        out_shape=jax.ShapeDtypeStruct((n, d), F32),
        scratch_shapes=[pltpu.VMEM((tj, d), F32)],
        compiler_params=_cparams(("arbitrary", "arbitrary", "arbitrary")),
        name="moe_combine",
    )(ys, code_t, gate_t, x, modrows, g.reshape(1, d), b.reshape(1, d))


def _moe(x, modrows, w_router, w1, w3, w2, layer, g, b, *, seg_rows, row0, row_stride):
    n, d = x.shape
    nseg = n // seg_rows
    u, code, gate = _router(x, modrows, w_router, seg_rows=seg_rows, row0=row0, row_stride=row_stride)
    code_e = code.transpose(0, 2, 1, 3)
    code_t = code.transpose(0, 1, 3, 2).reshape(nseg, seg_rows, N_EXPERTS)
    gate_t = gate.transpose(0, 1, 3, 2).reshape(nseg, seg_rows, N_EXPERTS)
    xs = _gather(u, code_e, seg_rows=seg_rows)
    cap = xs.shape[2]
    ys = _ffn(xs.reshape(N_EXPERTS, nseg * cap, d), w1, w3, w2, layer).reshape(N_EXPERTS, nseg, cap, d)
    return _combine(ys, code_t, gate_t, x, modrows, g, b, seg_rows=seg_rows, row0=row0,
                    row_stride=row_stride)


def kernel(x, c, ctx, c_ctx, w_mod, b_mod, w_in, w_out, ln_g, ln_b, gdn_conv, gdn_a_log, gdn_dt_bias,
           gdn_norm, gla_w_gate, gla_b_gate, gla_norm, hgrn_gamma, hgrn_norm, na_rpb,
           moe_router, moe_w1, moe_w3, moe_w2):
    batch, seq, d = x.shape
    n_ctx = ctx.shape[1]
    xl = x.reshape(batch * seq, d)
    xc = ctx.reshape(batch * n_ctx, d)

    cc = jnp.zeros((8, d), F32).at[:batch].set(c).at[batch].set(c_ctx)
    mods = _modulation(cc, w_mod, b_mod)

    w_in_r = _regroup_w_in(w_in)
    w_out_b = w_out.astype(BF16)
    w1_b, w3_b, w2_b = moe_w1.astype(BF16), moe_w3.astype(BF16), moe_w2.astype(BF16)
    low = jnp.asarray(_LOW_MATS, BF16)

    for l in range(DEPTH):
        keep_ctx = l < DEPTH - 1
        modrows = mods[l].reshape(8 * 6, 1, d)
        p_lat = _inproj(xl, modrows, w_in_r, l, seg_rows=seq, row0=0)
        p_ctx = _inproj(xc, modrows, w_in_r, l, seg_rows=batch * n_ctx, row0=batch)
        mixes = [
            _gdn(p_lat, p_ctx, gdn_conv[l], gdn_a_log[l], gdn_dt_bias[l], gdn_norm[l], batch=batch),
            _gla(p_lat, p_ctx, gla_w_gate[l], gla_b_gate[l], gla_norm[l], low, batch=batch),
            _hgrn(p_lat, p_ctx, hgrn_gamma, hgrn_norm[l], low, batch=batch, layer=l),
            _na(p_lat, p_ctx, na_rpb[l], batch=batch),
        ]
        xl = _outproj([m[0] for m in mixes], w_out_b, l, xl, modrows, ln_g[l, 0], ln_b[l, 0],
                      seg_rows=seq, row0=0)
        if keep_ctx:
            xc = _outproj([m[1] for m in mixes], w_out_b, l, xc, modrows, ln_g[l, 0], ln_b[l, 0],
                          seg_rows=batch * n_ctx, row0=batch)
        moe_args = (moe_router[l], w1_b, w3_b, w2_b, l, ln_g[l, 1], ln_b[l, 1])
        xl = _moe(xl, modrows, *moe_args, seg_rows=seq, row0=0, row_stride=1)
        if keep_ctx:
            xc = _moe(xc, modrows, *moe_args, seg_rows=n_ctx, row0=batch, row_stride=0)
    return xl.reshape(batch, seq, d)
```

```python
import functools

import numpy as np
import jax
import jax.numpy as jnp
from jax import lax
from jax.experimental import pallas as pl
from jax.experimental.pallas import tpu as pltpu

F32 = jnp.float32
BF16 = jnp.bfloat16
I32 = jnp.int32

D_MODEL = 2048
DEPTH = 4
GRID_W = 64
HEAD_DIM = 128
N_GROUPS = 4
GROUP_WIDTH = D_MODEL // N_GROUPS
GROUP_HEADS = GROUP_WIDTH // HEAD_DIM
GDN_CONV = 5
GDN_CHUNK = 64
ROWS = 256
GLA_DK = HEAD_DIM // 2
GLA_RANK = 16
GLA_TAU = 16.0
NA_ROWS = 8
NA_COLS = 16
ROPE_BASE = 10000.0
N_EXPERTS = 16
EC_CAPACITY = 2
D_EXPERT = D_MODEL // 2
DEEPNORM_ALPHA = (2 * DEPTH) ** 0.25
LN_EPS = 1e-5
NORM_EPS = 1e-6
NEG_BIG = -1e30

LANES = 128
SUBLANES = 8
VMEM_LIMIT = 56 * 1024 * 1024

_OFF = {}
_o = 0
for _name, _w in (("a_q", 512), ("a_k", 512), ("a_v", 512), ("a_z", 512), ("a_beta", 8), ("a_dec", 8),
                  ("b_q", 256), ("b_k", 256), ("b_v", 512), ("b_r", 512), ("b_code", 32),
                  ("c_q", 512), ("c_i", 512), ("c_g", 512), ("c_f", 1024),
                  ("d_q", 512), ("d_k", 512), ("d_v", 512)):
    _OFF[_name] = _o
    _o += _w
N_IN = _o

BLK = dict(a_q=0, a_k=4, a_v=8, a_z=12, c_q=16, c_i=20, c_g=24, c_f0=28, c_f1=32,
           d_q=36, d_k=40, d_v=44, b_v=48, b_r=52, b_q=56, b_k=60, small=64)
NP_BLOCKS = 65
NP = NP_BLOCKS * LANES
SM_BETA, SM_DEC, SM_CODE = 0, 8, 16


def _regroup_w_in(w_in, dtype=BF16):
    def cols(a, b):
        return w_in[..., a:b]

    def heads_padded(name):
        w = cols(_OFF[name], _OFF[name] + GROUP_HEADS * GLA_DK)
        w = w.reshape(w.shape[:-1] + (GROUP_HEADS, GLA_DK))
        w = jnp.pad(w, [(0, 0)] * (w.ndim - 1) + [(0, LANES - GLA_DK)])
        return w.reshape(w.shape[:-2] + (GROUP_HEADS * LANES,))

    small = jnp.concatenate([cols(_OFF["a_beta"], _OFF["a_beta"] + 16), cols(_OFF["b_code"], _OFF["b_code"] + 32)],
                            axis=-1)
    small = jnp.pad(small, [(0, 0)] * (small.ndim - 1) + [(0, LANES - 48)])
    parts = [cols(_OFF["a_q"], _OFF["a_q"] + 2048),
             cols(_OFF["c_q"], _OFF["c_q"] + 2560),
             cols(_OFF["d_q"], _OFF["d_q"] + 1536),
             cols(_OFF["b_v"], _OFF["b_v"] + 1024),
             heads_padded("b_q"), heads_padded("b_k"), small]
    return jnp.concatenate(parts, axis=-1).astype(dtype)


def _cparams(sem):
    return pltpu.CompilerParams(dimension_semantics=sem, vmem_limit_bytes=VMEM_LIMIT)


def _sigmoid(x):
    return 1.0 / (1.0 + jnp.exp(-x))


def _silu(x):
    return x * _sigmoid(x)


def _split2(x):
    hi = x.astype(BF16)
    lo = (x - hi.astype(F32)).astype(BF16)
    return hi, lo


def _split3(x):
    hi = x.astype(BF16)
    r = x - hi.astype(F32)
    mid = r.astype(BF16)
    lo = (r - mid.astype(F32)).astype(BF16)
    return hi, mid, lo


def _mm(a, b):
    return jnp.dot(a, b, preferred_element_type=F32)


def _mm_nt(a, b):
    return lax.dot_general(a, b, (((1,), (1,)), ((), ())), preferred_element_type=F32)


def _mm_tn(a, b):
    return lax.dot_general(a, b, (((0,), (0,)), ((), ())), preferred_element_type=F32)


def _mm_x2(a, b):
    ah, al = _split2(a)
    bh, bl = _split2(b)
    return _mm(ah, bh) + _mm(ah, bl) + _mm(al, bh)


def _mm_exact_lhs(a_bf, b):
    bh, bm, bl = _split3(b)
    return _mm(a_bf, bh) + _mm(a_bf, bm) + _mm(a_bf, bl)


def _mm_exact_rhs(a, b_bf):
    ah, am, al = _split3(a)
    return _mm(ah, b_bf) + _mm(am, b_bf) + _mm(al, b_bf)


def _layer_stats(x):
    mu = jnp.mean(x, axis=-1, keepdims=True)
    xc = x - mu
    var = jnp.mean(xc * xc, axis=-1, keepdims=True)
    return xc * lax.rsqrt(var + LN_EPS)


def _mod_kernel(c_ref, w_ref, b_ref, o_ref):
    s = _silu(c_ref[...])
    hi, lo = _split2(s)
    w = w_ref[0].astype(BF16)
    o_ref[0] = _mm(hi, w) + _mm(lo, w) + b_ref[0]


def _modulation(cc, w_mod, b_mod):
    depth, d, n6 = w_mod.shape
    tn = 1024
    return pl.pallas_call(
        _mod_kernel,
        grid=(depth, n6 // tn),
        in_specs=[pl.BlockSpec((8, d), lambda l, j: (0, 0)),
                  pl.BlockSpec((1, d, tn), lambda l, j: (l, 0, j)),
                  pl.BlockSpec((1, 1, tn), lambda l, j: (l, 0, j))],
        out_specs=pl.BlockSpec((1, 8, tn), lambda l, j: (l, 0, j)),
        out_shape=jax.ShapeDtypeStruct((depth, 8, n6), F32),
        compiler_params=_cparams(("arbitrary", "arbitrary")),
        name="modulation",
    )(cc, w_mod, b_mod.reshape(depth, 1, n6))


def _inproj_kernel(x_ref, sh_ref, sc_ref, w_ref, o_ref, u_scr):
    @pl.when(pl.program_id(1) == 0)
    def _():
        y = _layer_stats(x_ref[...]) * (1.0 + sc_ref[0]) + sh_ref[0]
        u_scr[...] = y.astype(BF16)

    o_ref[...] = _mm(u_scr[...], w_ref[0])


def _inproj(x, modrows, w, layer, *, seg_rows, row0):
    n, d = x.shape
    tm = min(512, n, seg_rows)
    tn = 13 * LANES
    tiles_per_seg = seg_rows // tm

    def mrow(k):
        return lambda i, j: ((row0 + i // tiles_per_seg) * 6 + k, 0, 0)

    return pl.pallas_call(
        _inproj_kernel,
        grid=(n // tm, NP // tn),
        in_specs=[pl.BlockSpec((tm, d), lambda i, j: (i, 0)),
                  pl.BlockSpec((1, 1, d), mrow(0)),
                  pl.BlockSpec((1, 1, d), mrow(1)),
                  pl.BlockSpec((1, d, tn), lambda i, j: (layer, 0, j))],
        out_specs=pl.BlockSpec((tm, tn), lambda i, j: (i, j)),
        out_shape=jax.ShapeDtypeStruct((n, NP), F32),
        scratch_shapes=[pltpu.VMEM((tm, d), BF16)],
        compiler_params=_cparams(("arbitrary", "arbitrary")),
        name="inproj",
    )(x, modrows, modrows, w)


def _outproj_kernel(m0, m1, m2, m3, w_ref, x_ref, gate_ref, g_ref, b_ref, o_ref):
    gw = GROUP_WIDTH
    acc = _mm(m0[...], w_ref[0, 0 * gw:1 * gw, :])
    acc += _mm(m1[...], w_ref[0, 1 * gw:2 * gw, :])
    acc += _mm(m2[...], w_ref[0, 2 * gw:3 * gw, :])
    acc += _mm(m3[...], w_ref[0, 3 * gw:4 * gw, :])
    y = DEEPNORM_ALPHA * x_ref[...] + gate_ref[0] * acc
    o_ref[...] = _layer_stats(y) * g_ref[...] + b_ref[...]


def _outproj(mixes, w, layer, x, modrows, g, b, *, seg_rows, row0):
    n, d = x.shape
    tm = min(512, n, seg_rows)
    tiles_per_seg = seg_rows // tm
    mspec = pl.BlockSpec((tm, GROUP_WIDTH), lambda i: (i, 0))
    return pl.pallas_call(
        _outproj_kernel,
        grid=(n // tm,),
        in_specs=[mspec, mspec, mspec, mspec,
                  pl.BlockSpec((1, d, d), lambda i: (layer, 0, 0)),
                  pl.BlockSpec((tm, d), lambda i: (i, 0)),
                  pl.BlockSpec((1, 1, d), lambda i: ((row0 + i // tiles_per_seg) * 6 + 2, 0, 0)),
                  pl.BlockSpec((1, d), lambda i: (0, 0)),
                  pl.BlockSpec((1, d), lambda i: (0, 0))],
        out_specs=pl.BlockSpec((tm, d), lambda i: (i, 0)),
        out_shape=jax.ShapeDtypeStruct((n, d), F32),
        compiler_params=_cparams(("arbitrary",)),
        name="outproj",
    )(*mixes, w, x, modrows, g.reshape(1, d), b.reshape(1, d))


def _bwd_chunk_index(i, n_ctx_chunks, n_chunks):
    return jnp.where(i < n_ctx_chunks, n_ctx_chunks - 1 - i, n_chunks + n_ctx_chunks - 1 - i)


def _rms_gate_store(of_s, ob_s, g_ref, gate_fn, gate_refs, out_refs, seg_bounds):
    for (lo, hi), gate_ref, out_ref in zip(seg_bounds, gate_refs, out_refs):
        n = hi - lo
        tile = min(512, n)
        for r0 in range(0, n, tile):
            o = of_s[lo + r0: lo + r0 + tile, :] + ob_s[lo + r0: lo + r0 + tile, :]
            y = o * lax.rsqrt(jnp.mean(o * o, axis=-1, keepdims=True) + NORM_EPS) * g_ref[...]
            out_ref[r0:r0 + tile, :] = (y * gate_fn(gate_ref[r0:r0 + tile, :])).astype(out_ref.dtype)


def _select_col(x, lane, c):
    return jnp.sum(jnp.where(lane == c, x, 0.0), axis=1, keepdims=True)


def _rows8(row):
    return jnp.broadcast_to(row, (SUBLANES, LANES))


def _gdn_kernel(ql, kl, vl, zl, sl, qc, kc, vc, zc, sc, wq, wk, wv, alog, dtb, gn,
                ol, oc, q_s, k_s, v_s, tok_s, pad_s, u_s, w_s, qg_s, kd_s, p_s, ge_s, *, n_ctx, n_lat):
    h = pl.program_id(1)
    n_tot = n_ctx + n_lat
    segs = ((0, n_ctx), (n_ctx, n_tot))

    def conv_into(x_ref, w_ref, dst, lo, n, l2_scale):
        pad_s[0:8, :] = jnp.zeros((8, LANES), F32)
        pad_s[8:8 + n, :] = x_ref[...]
        pad_s[8 + n:16 + n, :] = jnp.zeros((8, LANES), F32)
        tile = min(512, n)
        for r0 in range(0, n, tile):
            acc = jnp.zeros((tile, LANES), F32)
            for i in range(GDN_CONV):
                s0 = 8 + r0 + i - GDN_CONV // 2
                acc = acc + pad_s[s0:s0 + tile, :] * w_ref[i:i + 1, :]
            y = _silu(acc)
            if l2_scale is not None:
                y = y * (lax.rsqrt(jnp.sum(y * y, axis=-1, keepdims=True) + NORM_EPS) * l2_scale)
            dst[lo + r0: lo + r0 + tile, :] = y

    for (lo, hi), (xq, xk, xv) in zip(segs, ((qc, kc, vc), (ql, kl, vl))):
        conv_into(xq, wq, q_s, lo, hi - lo, HEAD_DIM ** -0.5)
        conv_into(xk, wk, k_s, lo, hi - lo, 1.0)
        conv_into(xv, wv, v_s, lo, hi - lo, None)

    def tok_into(s_ref, lo, n):
        tile = min(512, n)
        lane = lax.broadcasted_iota(I32, (tile, LANES), 1)
        for r0 in range(0, n, tile):
            sm = s_ref[r0:r0 + tile, :]
            cols = []
            for d in range(2):
                beta = _sigmoid(_select_col(sm, lane, SM_BETA + d * GROUP_HEADS + h))
                dec = _select_col(sm, lane, SM_DEC + d * GROUP_HEADS + h)
                a_neg = -jnp.exp(_select_col(alog[...], lane[0:1], d * GROUP_HEADS + h))
                bias = _select_col(dtb[...], lane[0:1], d * GROUP_HEADS + h)
                la = a_neg * jax.nn.softplus(dec + bias)
                cols += [beta, la]
            t = jnp.where(lane == 0, cols[0], jnp.where(lane == 1, cols[1],
                          jnp.where(lane == 2, cols[2], jnp.where(lane == 3, cols[3], 0.0))))
            tok_s[lo + r0: lo + r0 + tile, :] = t

    tok_into(sc, 0, n_ctx)
    tok_into(sl, n_ctx, n_lat)

    c = GDN_CHUNK
    nb = ROWS // c
    r = lax.broadcasted_iota(I32, (ROWS, ROWS), 0)
    cc = lax.broadcasted_iota(I32, (ROWS, ROWS), 1)
    same = (r // c) == (cc // c)
    same_bf = same.astype(BF16)
    eye = r == cc
    eye_f = eye.astype(F32)

    def inv_unit(a):
        t = eye_f - a
        pb = a.astype(BF16)
        for _ in range(5):
            pb = _mm(pb, pb).astype(BF16)
            t = t + _mm(t.astype(BF16), pb)
        return t

    def phase1(si, _):
        off = pl.multiple_of(si * ROWS, ROWS)
        q = q_s[pl.ds(off, ROWS), :]
        k = k_s[pl.ds(off, ROWS), :]
        v = v_s[pl.ds(off, ROWS), :]
        tk = tok_s[pl.ds(off, ROWS), :]
        qb = q.astype(BF16)
        kb = k.astype(BF16)
        kk = _mm_nt(kb, kb)
        qk = _mm_nt(qb, kb)
        for d, rev in ((0, False), (1, True)):
            beta = tk[:, 2 * d:2 * d + 1]
            la = tk[:, 2 * d + 1:2 * d + 2]
            incl = same & ((cc >= r) if rev else (cc <= r))
            strict = same & ((cc > r) if rev else (cc < r))
            gi_full = _mm_exact_lhs(incl.astype(BF16), jnp.broadcast_to(la, (ROWS, LANES)))
            gi = jnp.concatenate([gi_full, gi_full], axis=1)
            gj = _mm_exact_lhs(same_bf, jnp.where(eye, gi, 0.0))
            diff = gi - gj
            a = beta * kk * jnp.exp(jnp.where(strict, diff, -jnp.inf))
            p = qk * jnp.exp(jnp.where(incl, diff, -jnp.inf))
            for kb_ in range(nb):
                p_s[d, pl.ds(pl.multiple_of(off + kb_ * c, c), c), :] = (
                    p[kb_ * c:(kb_ + 1) * c, kb_ * c:(kb_ + 1) * c].astype(BF16))
            eg = jnp.exp(gi_full)
            rhs = jnp.concatenate([beta * v, (beta * eg) * k], axis=1)
            sol = _mm_x2(inv_unit(a), rhs)
            u_s[d, pl.ds(off, ROWS), :] = sol[:, :LANES]
            w_s[d, pl.ds(off, ROWS), :] = sol[:, LANES:].astype(BF16)
            gr = gi_full.reshape(nb, c, LANES)
            gl = gr[:, 0:1, :] if rev else gr[:, c - 1:c, :]
            g_last = jnp.broadcast_to(gl, (nb, c, LANES)).reshape(ROWS, LANES)
            qg_s[d, pl.ds(off, ROWS), :] = (q * eg).astype(BF16)
            kd_s[d, pl.ds(off, ROWS), :] = (k * jnp.exp(g_last - gi_full)).astype(BF16)
            ge = jnp.broadcast_to(jnp.exp(gl), (nb, SUBLANES, LANES)).reshape(nb * SUBLANES, LANES)
            ge_s[d, pl.ds(pl.multiple_of(si * nb * SUBLANES, nb * SUBLANES), nb * SUBLANES), :] = ge
        return 0

    lax.fori_loop(0, n_tot // ROWS, phase1, 0)

    of_s, ob_s = q_s, k_s
    n_cc = n_ctx // c
    n_ch = n_tot // c

    def step(i, carry):
        new = []
        for d, (st, ch) in enumerate(zip(carry, (i, _bwd_chunk_index(i, n_cc, n_ch)))):
            off = pl.multiple_of(ch * c, c)
            w = w_s[d, pl.ds(off, c), :]
            qg = qg_s[d, pl.ds(off, c), :]
            ge = ge_s[d, pl.ds(pl.multiple_of(ch * SUBLANES, SUBLANES), SUBLANES), :][0:1, :]
            stb = st.astype(BF16)
            r1 = _mm(jnp.concatenate([w, qg], axis=0), stb)
            vb = (u_s[d, pl.ds(off, c), :] - r1[:c]).astype(BF16)
            o = r1[c:] + _mm(p_s[d, pl.ds(off, c), :], vb)
            new.append(ge * st + _mm_tn(kd_s[d, pl.ds(off, c), :], vb))
            (ob_s if d else of_s)[pl.ds(off, c), :] = o
        return tuple(new)

    z0 = jnp.zeros((HEAD_DIM, HEAD_DIM), F32)
    lax.fori_loop(0, n_ch, step, (z0, z0))

    _rms_gate_store(of_s, ob_s, gn, _silu, (zc, zl), (oc, ol), segs)


def _gdn(p_lat, p_ctx, conv_w, a_log, dt_bias, gnorm, *, batch):
    n_lat = p_lat.shape[0] // batch
    n_ctx = p_ctx.shape[0] // batch
    n_tot = n_lat + n_ctx
    assert n_lat % ROWS == 0 and n_ctx % ROWS == 0
    n_ch = n_tot // GDN_CHUNK

    def blk(n, base):
        return pl.BlockSpec((n, LANES), lambda b, h: (b, base + h))

    def small(n):
        return pl.BlockSpec((n, LANES), lambda b, h: (b, BLK["small"]))

    def cw(base):
        return pl.BlockSpec((8, LANES), lambda b, h: (0, base + h))

    row = pl.BlockSpec((1, LANES), lambda b, h: (0, 0))
    conv_p = jnp.zeros((8, 3 * GROUP_WIDTH), F32).at[:GDN_CONV].set(conv_w)
    alog_row = jnp.zeros((1, LANES), F32).at[0, :2 * GROUP_HEADS].set(a_log.reshape(-1))
    dtb_row = jnp.zeros((1, LANES), F32).at[0, :2 * GROUP_HEADS].set(dt_bias.reshape(-1))
    lat_in = [blk(n_lat, BLK[k]) for k in ("a_q", "a_k", "a_v", "a_z")] + [small(n_lat)]
    ctx_in = [blk(n_ctx, BLK[k]) for k in ("a_q", "a_k", "a_v", "a_z")] + [small(n_ctx)]
    kern = functools.partial(_gdn_kernel, n_ctx=n_ctx, n_lat=n_lat)
    return pl.pallas_call(
        kern,
        grid=(batch, GROUP_HEADS),
        in_specs=lat_in + ctx_in + [cw(0), cw(4), cw(8), row, row, row],
        out_specs=[pl.BlockSpec((n_lat, LANES), lambda b, h: (b, h)),
                   pl.BlockSpec((n_ctx, LANES), lambda b, h: (b, h))],
        out_shape=[jax.ShapeDtypeStruct((batch * n_lat, GROUP_WIDTH), BF16),
                   jax.ShapeDtypeStruct((batch * n_ctx, GROUP_WIDTH), BF16)],
        scratch_shapes=[pltpu.VMEM((n_tot, LANES), F32)] * 4
        + [pltpu.VMEM((n_lat + 16, LANES), F32),
           pltpu.VMEM((2, n_tot, LANES), F32),
           pltpu.VMEM((2, n_tot, LANES), BF16),
           pltpu.VMEM((2, n_tot, LANES), BF16),
           pltpu.VMEM((2, n_tot, LANES), BF16),
           pltpu.VMEM((2, n_tot, GDN_CHUNK), BF16),
           pltpu.VMEM((2, n_ch * SUBLANES, LANES), F32)],
        compiler_params=_cparams(("arbitrary", "arbitrary")),
        name="gdn",
    )(*([p_lat] * 5), *([p_ctx] * 5), conv_p, conv_p, conv_p, alog_row, dtb_row, gnorm.reshape(1, LANES))


DROWS = 128
N_LEVELS = 7


def _cumsum_matrices():
    i = np.arange(DROWS)[:, None]
    u = np.arange(DROWS)[None, :]
    return np.stack([(u <= i), (u >= i)]).astype(np.float32)


_LOW_MATS = _cumsum_matrices()


def _level_index(rev):
    r = lax.broadcasted_iota(I32, (DROWS, DROWS), 0)
    cc = lax.broadcasted_iota(I32, (DROWS, DROWS), 1)
    lev = 31 - lax.clz(r ^ cc)
    earlier = (cc > r) if rev else (cc < r)
    return jnp.where(earlier, lev, jnp.where(r == cc, -1, -2))


def _rows_at(b, rowi, m, shifts, cache):
    c = b.shape[0]
    out = jnp.zeros_like(b)
    for mval, sh in enumerate(shifts):
        if sh not in cache:
            cache[sh] = b if sh == 0 else pltpu.roll(b, sh % c, 0)
        src = rowi - sh
        cand = jnp.where((src >= 0) & (src < c), cache[sh], 0.0)
        out = jnp.where(m == mval, cand, out)
    return out


def _diag_phase1(q, k, v_bf, la, rev, low_ref, lev):
    c = DROWS
    b = _mm_exact_lhs(low_ref[1 if rev else 0], la)
    rowi = lax.broadcasted_iota(I32, (c, LANES), 0)
    win = {0: la}
    wout = {}
    rolled = {}
    for lv in (1, 2):
        s = 1 << lv
        m = rowi & (s - 1)
        if rev:
            win[lv] = b - _rows_at(b, rowi, m, [-(s - t) for t in range(s)], rolled)
            wout[lv] = _rows_at(b, rowi, m, list(range(s)), rolled) - b
        else:
            win[lv] = b - _rows_at(b, rowi, m, [t + 1 for t in range(s)], rolled)
            wout[lv] = _rows_at(b, rowi, m, [-(s - 1 - t) for t in range(s)], rolled) - b
    for lv in range(3, N_LEVELS + 1):
        s = 1 << lv
        nblk = c // s
        br = b.reshape(nblk, s, LANES)
        edge = br[:, 0:1, :] if rev else br[:, s - 1:s, :]
        zero = jnp.zeros((1, 1, LANES), F32)
        if nblk == 1:
            before = zero
        elif rev:
            before = jnp.concatenate([edge[1:], zero], axis=0)
        else:
            before = jnp.concatenate([zero, edge[:-1]], axis=0)
        win[lv] = (br - before).reshape(c, LANES)
        wout[lv] = (edge - br).reshape(c, LANES)
    qb = q.astype(BF16)
    kb = k.astype(BF16)
    p = jnp.where(lev == -1, _mm_nt(qb, kb), 0.0)
    for lv in range(N_LEVELS):
        qs = (q * jnp.exp(win[lv])).astype(BF16)
        ks = kb if lv == 0 else (k * jnp.exp(wout[lv])).astype(BF16)
        p = jnp.where(lev == lv, _mm_nt(qs, ks), p)
    o_intra = _mm(p.astype(BF16), v_bf)
    qg = (q * jnp.exp(win[N_LEVELS])).astype(BF16)
    kd = (k * jnp.exp(wout[N_LEVELS])).astype(BF16)
    a_end = jnp.exp(b[0:1, :] if rev else b[c - 1:c, :])
    return o_intra, qg, kd, a_end


def _diag_store_phase1(vals, off, ci, low_ref, levs, v_bs, o_s, qg_s, kd_s, ae_s):
    q, kf, kb, v, laf, lab = vals
    v_bf = v.astype(BF16)
    v_bs[pl.ds(off, DROWS), :] = v_bf
    for d, (k, la) in enumerate(((kf, laf), (kb, lab))):
        o_intra, qg, kd, a_end = _diag_phase1(q, k, v_bf, la, bool(d), low_ref, levs[d])
        o_s[d][pl.ds(off, DROWS), :] = o_intra
        qg_s[d, pl.ds(off, DROWS), :] = qg
        kd_s[d, pl.ds(off, DROWS), :] = kd
        ae_s[d, pl.ds(pl.multiple_of(ci * SUBLANES, SUBLANES), SUBLANES), :] = _rows8(a_end)


def _diag_phase2(v_bs, o_s, qg_s, kd_s, ae_s, n_ctx, n_tot):
    n_cc = n_ctx // DROWS
    n_ch = n_tot // DROWS

    def step(i, carry):
        new = []
        for d, (st, ch) in enumerate(zip(carry, (i, _bwd_chunk_index(i, n_cc, n_ch)))):
            off = pl.multiple_of(ch * DROWS, DROWS)
            a_end = ae_s[d, pl.ds(pl.multiple_of(ch * SUBLANES, SUBLANES), SUBLANES), :][0:1, :]
            o_s[d][pl.ds(off, DROWS), :] += _mm_nt(qg_s[d, pl.ds(off, DROWS), :], st.astype(BF16))
            new.append(st * a_end + _mm_tn(v_bs[pl.ds(off, DROWS), :], kd_s[d, pl.ds(off, DROWS), :]))
        return tuple(new)

    z0 = jnp.zeros((HEAD_DIM, HEAD_DIM), F32)
    lax.fori_loop(0, n_ch, step, (z0, z0))


def _diag_scratch(n_tot):
    n_ch = n_tot // DROWS
    return [pltpu.VMEM((n_tot, LANES), BF16),
            pltpu.VMEM((n_tot, LANES), F32),
            pltpu.VMEM((n_tot, LANES), F32),
            pltpu.VMEM((2, n_tot, LANES), BF16),
            pltpu.VMEM((2, n_tot, LANES), BF16),
            pltpu.VMEM((2, n_ch * SUBLANES, LANES), F32)]


def _gla_kernel(ql, kl, vl, rl, sl, qc, kc, vc, rc, sc, cos_ref, sin_ref, perm_ref, wg_ref, bg_ref, gn,
                low_ref, ol, oc, v_bs, of_s, ob_s, qg_s, kd_s, ae_s, *, n_ctx, n_lat):
    n_tot = n_ctx + n_lat
    segs = ((0, n_ctx), (n_ctx, n_tot))
    levs = (_level_index(False), _level_index(True))

    for (lo, hi), (xq, xk, xv, xs), rope in zip(segs, ((qc, kc, vc, sc), (ql, kl, vl, sl)), (False, True)):
        def chunk(ci, _, lo=lo, xq=xq, xk=xk, xv=xv, xs=xs, rope=rope):
            r0 = pl.multiple_of(ci * DROWS, DROWS)
            q = xq[pl.ds(r0, DROWS), :]
            k = xk[pl.ds(r0, DROWS), :]
            if rope:
                cs = cos_ref[pl.ds(r0, DROWS), :]
                sn = sin_ref[pl.ds(r0, DROWS), :]
                q = q * cs + _mm_exact_rhs(q, perm_ref[...]) * sn
                k = k * cs + _mm_exact_rhs(k, perm_ref[...]) * sn
            sm = xs[pl.ds(r0, DROWS), :]
            las = [jax.nn.log_sigmoid(_mm_x2(sm, wg_ref[d, 0]) + bg_ref[d, 0]) * (1.0 / GLA_TAU) for d in range(2)]
            vals = (q * (GLA_DK ** -0.5), k, k, xv[pl.ds(r0, DROWS), :], las[0], las[1])
            _diag_store_phase1(vals, pl.multiple_of(lo + r0, DROWS), lo // DROWS + ci, low_ref, levs,
                               v_bs, (of_s, ob_s), qg_s, kd_s, ae_s)
            return 0

        lax.fori_loop(0, (hi - lo) // DROWS, chunk, 0, unroll=2)

    _diag_phase2(v_bs, (of_s, ob_s), qg_s, kd_s, ae_s, n_ctx, n_tot)
    _rms_gate_store(of_s, ob_s, gn, _silu, (rc, rl), (oc, ol), segs)


def _rope_tables(n_lat):
    t = np.arange(n_lat)
    nf = GLA_DK // 4
    inv = (ROPE_BASE ** (-jnp.arange(nf, dtype=F32) / nf))
    lane = np.arange(LANES)
    f = lane % nf
    use_col = (lane % GLA_DK) >= GLA_DK // 2
    first = (lane % (2 * nf)) < nf
    real = lane < GLA_DK
    pos = jnp.where(use_col[None, :], (t % GRID_W)[:, None], (t // GRID_W)[:, None]).astype(F32)
    ang = pos * inv[f][None, :]
    cos = jnp.where(real[None, :], jnp.cos(ang), 1.0)
    sin = jnp.where(real[None, :], jnp.where(first[None, :], -jnp.sin(ang), jnp.sin(ang)), 0.0)
    partner = np.where(first, lane + nf, lane - nf)
    perm = np.zeros((LANES, LANES), np.float32)
    perm[partner[real], lane[real]] = 1.0
    return cos.astype(F32), sin.astype(F32), jnp.asarray(perm, BF16)


def _gla(p_lat, p_ctx, w_gate, b_gate, gnorm, low, *, batch):
    n_lat = p_lat.shape[0] // batch
    n_ctx = p_ctx.shape[0] // batch
    n_tot = n_lat + n_ctx
    assert n_lat % DROWS == 0 and n_ctx % DROWS == 0
    cos, sin, perm = _rope_tables(n_lat)
    wg = jnp.zeros((2, GROUP_HEADS, LANES, LANES), F32)
    wsrc = w_gate.reshape(2, GLA_RANK, GROUP_HEADS, GLA_DK).transpose(0, 2, 1, 3)
    for d in range(2):
        wg = wg.at[d, :, SM_CODE + d * GLA_RANK: SM_CODE + (d + 1) * GLA_RANK, :GLA_DK].set(wsrc[d])
    bg = jnp.zeros((2, GROUP_HEADS, 1, LANES), F32).at[:, :, 0, :GLA_DK].set(
        b_gate.reshape(2, GROUP_HEADS, GLA_DK))

    def blk(n, base):
        return pl.BlockSpec((n, LANES), lambda b, h: (b, base + h))

    def small(n):
        return pl.BlockSpec((n, LANES), lambda b, h: (b, BLK["small"]))

    def const2(shape):
        return pl.BlockSpec(shape, lambda b, h: (0, 0))

    names = ("b_q", "b_k", "b_v", "b_r")
    kern = functools.partial(_gla_kernel, n_ctx=n_ctx, n_lat=n_lat)
    return pl.pallas_call(
        kern,
        grid=(batch, GROUP_HEADS),
        in_specs=[blk(n_lat, BLK[k]) for k in names] + [small(n_lat)]
        + [blk(n_ctx, BLK[k]) for k in names] + [small(n_ctx)]
        + [const2((n_lat, LANES)), const2((n_lat, LANES)), const2((LANES, LANES)),
           pl.BlockSpec((2, 1, LANES, LANES), lambda b, h: (0, h, 0, 0)),
           pl.BlockSpec((2, 1, 1, LANES), lambda b, h: (0, h, 0, 0)),
           const2((1, LANES)),
           pl.BlockSpec(low.shape, lambda b, h: (0, 0, 0))],
        out_specs=[pl.BlockSpec((n_lat, LANES), lambda b, h: (b, h)),
                   pl.BlockSpec((n_ctx, LANES), lambda b, h: (b, h))],
        out_shape=[jax.ShapeDtypeStruct((batch * n_lat, GROUP_WIDTH), BF16),
                   jax.ShapeDtypeStruct((batch * n_ctx, GROUP_WIDTH), BF16)],
        scratch_shapes=_diag_scratch(n_tot),
        compiler_params=_cparams(("arbitrary", "arbitrary")),
        name="gla",
    )(*([p_lat] * 5), *([p_ctx] * 5), cos, sin, perm, wg, bg, gnorm.reshape(1, LANES), low)


def _hgrn_kernel(ql, il, gl, f0l, f1l, qc, ic, gc, f0c, f1c, gam_ref, gn, low_ref,
                 ol, oc, v_bs, of_s, ob_s, qg_s, kd_s, ae_s, *, n_ctx, n_lat, layer):
    n_tot = n_ctx + n_lat
    segs = ((0, n_ctx), (n_ctx, n_tot))
    levs = (_level_index(False), _level_index(True))
    lbs = []
    for d in range(2):
        gam = gam_ref[d]
        ex = jnp.exp(gam - jnp.max(gam, axis=0, keepdims=True))
        pr = ex / jnp.sum(ex, axis=0, keepdims=True)
        lb = jnp.zeros((1, LANES), F32)
        for m in range(1, layer + 1):
            lb = lb + pr[m:m + 1, :]
        lbs.append(lb)

    for (lo, hi), (xq, xi, xf0, xf1) in zip(segs, ((qc, ic, f0c, f1c), (ql, il, f0l, f1l))):
        def chunk(ci, _, lo=lo, xq=xq, xi=xi, xf0=xf0, xf1=xf1):
            r0 = pl.multiple_of(ci * DROWS, DROWS)
            fs = [lbs[d] + (1.0 - lbs[d]) * _sigmoid(xf[pl.ds(r0, DROWS), :]) for d, xf in enumerate((xf0, xf1))]
            vals = (xq[pl.ds(r0, DROWS), :], 1.0 - fs[0], 1.0 - fs[1], xi[pl.ds(r0, DROWS), :],
                    jnp.log(fs[0]), jnp.log(fs[1]))
            _diag_store_phase1(vals, pl.multiple_of(lo + r0, DROWS), lo // DROWS + ci, low_ref, levs,
                               v_bs, (of_s, ob_s), qg_s, kd_s, ae_s)
            return 0

        lax.fori_loop(0, (hi - lo) // DROWS, chunk, 0, unroll=2)

    _diag_phase2(v_bs, (of_s, ob_s), qg_s, kd_s, ae_s, n_ctx, n_tot)
    _rms_gate_store(of_s, ob_s, gn, _sigmoid, (gc, gl), (oc, ol), segs)


def _hgrn(p_lat, p_ctx, gamma, gnorm, low, *, batch, layer):
    n_lat = p_lat.shape[0] // batch
    n_ctx = p_ctx.shape[0] // batch
    n_tot = n_lat + n_ctx
    assert n_lat % DROWS == 0 and n_ctx % DROWS == 0

    def blk(n, base):
        return pl.BlockSpec((n, LANES), lambda b, h: (b, base + h))

    names = ("c_q", "c_i", "c_g", "c_f0", "c_f1")
    kern = functools.partial(_hgrn_kernel, n_ctx=n_ctx, n_lat=n_lat, layer=layer)
    return pl.pallas_call(
        kern,
        grid=(batch, GROUP_HEADS),
        in_specs=[blk(n_lat, BLK[k]) for k in names] + [blk(n_ctx, BLK[k]) for k in names]
        + [pl.BlockSpec((2, gamma.shape[1], LANES), lambda b, h: (0, 0, h)),
           pl.BlockSpec((1, LANES), lambda b, h: (0, 0)),
           pl.BlockSpec(low.shape, lambda b, h: (0, 0, 0))],
        out_specs=[pl.BlockSpec((n_lat, LANES), lambda b, h: (b, h)),
                   pl.BlockSpec((n_ctx, LANES), lambda b, h: (b, h))],
        out_shape=[jax.ShapeDtypeStruct((batch * n_lat, GROUP_WIDTH), BF16),
                   jax.ShapeDtypeStruct((batch * n_ctx, GROUP_WIDTH), BF16)],
        scratch_shapes=_diag_scratch(n_tot),
        compiler_params=_cparams(("arbitrary", "arbitrary")),
        name="hgrn",
    )(*([p_lat] * 5), *([p_ctx] * 5), gamma, gnorm.reshape(1, LANES), low)


NA_GROUP = 4
NA_WIN = NA_ROWS + NA_GROUP


def _na_kernel(ws_ref, pat_ref, ql, kl, vl, qc, kc, vc, bias_ref, ol, oc, *, n_groups, win):
    scale = HEAD_DIM ** -0.5
    kcb = kc[...].astype(BF16)
    vcb = vc[...].astype(BF16)
    w = GRID_W
    gq = NA_GROUP * w

    def group(g, _):
        qoff = pl.multiple_of(g * gq, gq)
        koff = pl.multiple_of(ws_ref[g] * w, w)
        q = ql[pl.ds(qoff, gq), :].astype(BF16)
        kb = kl[pl.ds(koff, win * w), :].astype(BF16)
        vb = vl[pl.ds(koff, win * w), :].astype(BF16)
        s_loc = _mm_nt(q, kb) * scale + bias_ref[0, pat_ref[g]]
        s_ctx = _mm_nt(q, kcb) * scale
        m = jnp.maximum(jnp.max(s_loc, axis=-1, keepdims=True), jnp.max(s_ctx, axis=-1, keepdims=True))
        p_loc = jnp.exp(s_loc - m)
        p_ctx = jnp.exp(s_ctx - m)
        den = jnp.sum(p_loc, axis=-1, keepdims=True) + jnp.sum(p_ctx, axis=-1, keepdims=True)
        o = _mm(p_loc.astype(BF16), vb) + _mm(p_ctx.astype(BF16), vcb)
        ol[pl.ds(qoff, gq), :] = (o / den).astype(ol.dtype)
        return 0

    lax.fori_loop(0, n_groups, group, 0, unroll=2 if n_groups % 2 == 0 else 1)

    s = _mm_nt(qc[...].astype(BF16), kcb) * scale
    p = jnp.exp(s - jnp.max(s, axis=-1, keepdims=True))
    o = _mm(p.astype(BF16), vcb) / jnp.sum(p, axis=-1, keepdims=True)
    oc[...] = o.astype(oc.dtype)


def _na_plan(n_rows):
    kr = min(NA_ROWS, n_rows)
    win = min(NA_WIN, n_rows)
    n_groups = n_rows // NA_GROUP
    ws = np.clip(np.arange(n_groups) * NA_GROUP - kr // 2, 0, n_rows - win)
    seen, pats, pat_of = {}, [], []
    for g in range(n_groups):
        rows = g * NA_GROUP + np.arange(NA_GROUP)
        rs = np.clip(rows - kr // 2, 0, n_rows - kr)
        krow = ws[g] + np.arange(win)
        ok = (krow[None, :] >= rs[:, None]) & (krow[None, :] < rs[:, None] + kr)
        dr = np.where(ok, krow[None, :] - rows[:, None] + NA_ROWS - 1, 0)
        key = (ok.tobytes(), dr.tobytes())
        if key not in seen:
            seen[key] = len(pats)
            pats.append((ok, dr))
        pat_of.append(seen[key])
    return ws.astype(np.int32), np.asarray(pat_of, np.int32), pats, win


def _na_bias(rpb, pats, win):
    n_h = rpb.shape[0]
    n_p = len(pats)
    ok = np.stack([p[0] for p in pats])
    dr = np.stack([p[1] for p in pats]).reshape(-1)
    row_sel = np.zeros((dr.size, 2 * NA_ROWS - 1), np.float32)
    row_sel[np.arange(dr.size), dr] = 1.0
    cq = np.arange(GRID_W)
    dc = np.clip(cq[None, :] - cq[:, None], -(NA_COLS - 1), NA_COLS - 1) + NA_COLS - 1
    col_start = np.clip(cq - NA_COLS // 2, 0, GRID_W - NA_COLS)
    col_in = (cq[None, :] >= col_start[:, None]) & (cq[None, :] < col_start[:, None] + NA_COLS)
    col_sel = np.zeros((2 * NA_COLS - 1, GRID_W * GRID_W), np.float32)
    col_sel[dc.reshape(-1), np.arange(GRID_W * GRID_W)] = 1.0
    hp = lax.Precision.HIGHEST
    b = jnp.einsum("xr,hrd->hxd", row_sel, rpb.astype(F32), precision=hp)
    b = jnp.einsum("hxd,dq->hxq", b, col_sel, precision=hp)
    b = b.reshape(n_h, n_p, NA_GROUP, win, GRID_W, GRID_W).transpose(0, 1, 2, 4, 3, 5)
    valid = ok[:, :, None, :, None] & col_in[None, None, :, None, :]
    b = jnp.where(valid[None], b, NEG_BIG)
    return b.reshape(n_h, n_p, NA_GROUP * GRID_W, win * GRID_W)


def _na(p_lat, p_ctx, rpb, *, batch):
    n_lat = p_lat.shape[0] // batch
    n_ctx = p_ctx.shape[0] // batch
    n_rows = n_lat // GRID_W
    assert n_rows % NA_GROUP == 0
    ws, pat_of, pats, win = _na_plan(n_rows)
    bias = _na_bias(rpb, pats, win)

    def blk(n, base):
        return pl.BlockSpec((n, LANES), lambda b, h, *_: (b, base + h))

    names = ("d_q", "d_k", "d_v")
    return pl.pallas_call(
        functools.partial(_na_kernel, n_groups=n_rows // NA_GROUP, win=win),
        grid_spec=pltpu.PrefetchScalarGridSpec(
            num_scalar_prefetch=2,
            grid=(batch, GROUP_HEADS),
            in_specs=[blk(n_lat, BLK[k]) for k in names] + [blk(n_ctx, BLK[k]) for k in names]
            + [pl.BlockSpec((1,) + bias.shape[1:], lambda b, h, *_: (h, 0, 0, 0))],
            out_specs=[pl.BlockSpec((n_lat, LANES), lambda b, h, *_: (b, h)),
                       pl.BlockSpec((n_ctx, LANES), lambda b, h, *_: (b, h))]),
        out_shape=[jax.ShapeDtypeStruct((batch * n_lat, GROUP_WIDTH), BF16),
                   jax.ShapeDtypeStruct((batch * n_ctx, GROUP_WIDTH), BF16)],
        compiler_params=_cparams(("arbitrary", "arbitrary")),
        name="na",
    )(jnp.asarray(ws), jnp.asarray(pat_of), *([p_lat] * 3), *([p_ctx] * 3), bias)


def _router_kernel(x_ref, sh_ref, sc_ref, wr_ref, u_ref, code_ref, gate_ref, cum_ref, lg_scr,
                   *, cap, n_tiles, tile):
    j = pl.program_id(1)
    u = _layer_stats(x_ref[...]) * (1.0 + sc_ref[0]) + sh_ref[0]
    u_ref[...] = u.astype(BF16)
    lg_scr[j] = lax.dot_general(wr_ref[...], u, (((1,), (1,)), ((), ())),
                                precision=lax.Precision.HIGHEST, preferred_element_type=F32)

    @pl.when(j == n_tiles - 1)
    def _():
        lg = lg_scr[...]
        ex = jnp.exp(lg - jnp.max(lg, axis=1, keepdims=True))
        aff = ex / jnp.sum(ex, axis=1, keepdims=True)
        bits = lax.bitcast_convert_type(aff, I32)

        def count(mask):
            per = jnp.sum(mask.astype(F32), axis=0)
            return jnp.sum(per, axis=1, keepdims=True)

        def bis(it, thr):
            cand = thr | jnp.left_shift(jnp.int32(1), 30 - it)
            ok = count(bits >= cand[None]) >= float(cap)
            return jnp.where(ok, cand, thr)

        thr = lax.fori_loop(0, 31, bis, jnp.zeros((N_EXPERTS, 1), I32))
        gt = bits > thr[None]
        eq = bits == thr[None]
        need = float(cap) - count(gt)
        r = lax.broadcasted_iota(I32, (tile, tile), 0)
        c = lax.broadcasted_iota(I32, (tile, tile), 1)
        upper = (r < c).astype(BF16)
        carry_eq = jnp.zeros((N_EXPERTS, 1), F32)
        carry_sel = jnp.zeros((N_EXPERTS, 1), F32)
        for t in range(n_tiles):
            eq_t = eq[t].astype(BF16)
            pre_eq = _mm(eq_t, upper) + carry_eq
            sel = gt[t] | (eq[t] & (pre_eq < need))
            sel_b = sel.astype(BF16)
            pos = _mm(sel_b, upper) + carry_sel
            code_ref[0, t] = jnp.where(sel, pos.astype(I32), -1)
            gate_ref[0, t] = aff[t]
            carry_eq = carry_eq + jnp.sum(eq_t.astype(F32), axis=1, keepdims=True)
            carry_sel = carry_sel + jnp.sum(sel_b.astype(F32), axis=1, keepdims=True)
            cum_ref[0, t] = jnp.broadcast_to(carry_sel, (N_EXPERTS, LANES)).astype(I32)


def _router(x, modrows, w_router, *, seg_rows, row0, row_stride):
    n, d = x.shape
    nseg = n // seg_rows
    tile = min(512, seg_rows)
    n_tiles = seg_rows // tile
    cap = EC_CAPACITY * seg_rows // N_EXPERTS
    kern = functools.partial(_router_kernel, cap=cap, n_tiles=n_tiles, tile=tile)
    return pl.pallas_call(
        kern,
        grid=(nseg, n_tiles),
        in_specs=[pl.BlockSpec((tile, d), lambda s, j: (s * n_tiles + j, 0)),
                  pl.BlockSpec((1, 1, d), lambda s, j: ((row0 + s * row_stride) * 6 + 3, 0, 0)),
                  pl.BlockSpec((1, 1, d), lambda s, j: ((row0 + s * row_stride) * 6 + 4, 0, 0)),
                  pl.BlockSpec((N_EXPERTS, d), lambda s, j: (0, 0))],
        out_specs=[pl.BlockSpec((tile, d), lambda s, j: (s * n_tiles + j, 0)),
                   pl.BlockSpec((1, n_tiles, N_EXPERTS, tile), lambda s, j: (s, 0, 0, 0)),
                   pl.BlockSpec((1, n_tiles, N_EXPERTS, tile), lambda s, j: (s, 0, 0, 0)),
                   pl.BlockSpec((1, n_tiles, N_EXPERTS, LANES), lambda s, j: (s, 0, 0, 0))],
        out_shape=[jax.ShapeDtypeStruct((n, d), BF16),
                   jax.ShapeDtypeStruct((nseg, n_tiles, N_EXPERTS, tile), I32),
                   jax.ShapeDtypeStruct((nseg, n_tiles, N_EXPERTS, tile), F32),
                   jax.ShapeDtypeStruct((nseg, n_tiles, N_EXPERTS, LANES), I32)],
        scratch_shapes=[pltpu.VMEM((n_tiles, N_EXPERTS, tile), F32)],
        compiler_params=_cparams(("arbitrary", "arbitrary")),
        name="router",
    )(x, modrows, modrows, w_router.T)


SLOT_TILE = 128


def _slot_tiles_touched(cum_ref, base, t, ts, n_slot_tiles):
    lo = cum_ref[base + t]
    hi = cum_ref[base + t + 1]
    return [(lo < (i + 1) * ts) & (hi > i * ts) for i in range(n_slot_tiles)]


def _gather_kernel(cum_ref, u_ref, code_ref, o_ref, acc, *, cap, n_tiles, tile, ts):
    base = (pl.program_id(0) * N_EXPERTS + pl.program_id(1)) * (n_tiles + 1)
    acc[...] = jnp.zeros_like(acc)
    slot = lax.broadcasted_iota(I32, (ts, tile), 0)
    for t in range(n_tiles):
        for i, touched in enumerate(_slot_tiles_touched(cum_ref, base, t, ts, cap // ts)):
            @pl.when(touched)
            def _(t=t, i=i):
                oh = (code_ref[0, 0, t:t + 1, :] == slot + i * ts).astype(BF16)
                acc[i * ts:(i + 1) * ts, :] += _mm(oh, u_ref[t * tile:(t + 1) * tile, :])
    o_ref[0, 0] = acc[...].astype(BF16)


def _gather(u, code_e, cum_flat, *, seg_rows):
    n, d = u.shape
    nseg, _, n_tiles, tile = code_e.shape
    cap = EC_CAPACITY * seg_rows // N_EXPERTS
    ts = min(SLOT_TILE, cap)
    return pl.pallas_call(
        functools.partial(_gather_kernel, cap=cap, n_tiles=n_tiles, tile=tile, ts=ts),
        grid_spec=pltpu.PrefetchScalarGridSpec(
            num_scalar_prefetch=1,
            grid=(nseg, N_EXPERTS),
            in_specs=[pl.BlockSpec((seg_rows, d), lambda s, e, *_: (s, 0)),
                      pl.BlockSpec((1, 1, n_tiles, tile), lambda s, e, *_: (s, e, 0, 0))],
            out_specs=pl.BlockSpec((1, 1, cap, d), lambda s, e, *_: (e, s, 0, 0)),
            scratch_shapes=[pltpu.VMEM((cap, d), F32)]),
        out_shape=jax.ShapeDtypeStruct((N_EXPERTS, nseg, cap, d), BF16),
        compiler_params=_cparams(("arbitrary", "arbitrary")),
        name="moe_gather",
    )(cum_flat, u, code_e)


def _ffn_kernel(x_ref, w1_ref, w3_ref, w2_ref, o_ref):
    x = x_ref[0]
    a = _mm(x, w1_ref[0, 0])
    g = _mm(x, w3_ref[0, 0])
    hid = (_silu(a) * g).astype(BF16)
    o_ref[0] = _mm(hid, w2_ref[0, 0]).astype(BF16)


def _ffn(xs, w1, w3, w2, layer):
    e, r, d = xs.shape
    f = w1.shape[3]
    tr = min(512, r)
    return pl.pallas_call(
        _ffn_kernel,
        grid=(e, r // tr),
        in_specs=[pl.BlockSpec((1, tr, d), lambda i, j: (i, j, 0)),
                  pl.BlockSpec((1, 1, d, f), lambda i, j: (layer, i, 0, 0)),
                  pl.BlockSpec((1, 1, d, f), lambda i, j: (layer, i, 0, 0)),
                  pl.BlockSpec((1, 1, f, d), lambda i, j: (layer, i, 0, 0))],
        out_specs=pl.BlockSpec((1, tr, d), lambda i, j: (i, j, 0)),
        out_shape=jax.ShapeDtypeStruct((e, r, d), BF16),
        compiler_params=_cparams(("arbitrary", "arbitrary")),
        name="moe_ffn",
    )(xs, w1, w3, w2)


def _combine_kernel(cum_ref, y_ref, code_ref, gate_ref, x_ref, m5_ref, g_ref, b_ref, o_ref, acc,
                    *, cap, n_tiles, ts):
    j = pl.program_id(1)
    e = pl.program_id(2)

    @pl.when(e == 0)
    def _():
        acc[...] = jnp.zeros_like(acc)

    lane = lax.broadcasted_iota(I32, code_ref.shape[1:], 1)
    code = jnp.sum(jnp.where(lane == e, code_ref[0], 0), axis=1, keepdims=True)
    gate = jnp.sum(jnp.where(lane == e, gate_ref[0], 0.0), axis=1, keepdims=True)
    slot = lax.broadcasted_iota(I32, (code.shape[0], ts), 1)
    base = (pl.program_id(0) * N_EXPERTS + e) * (n_tiles + 1)
    for i, touched in enumerate(_slot_tiles_touched(cum_ref, base, j, ts, cap // ts)):
        @pl.when(touched)
        def _(i=i):
            wm = jnp.where(code == slot + i * ts, gate, 0.0).astype(BF16)
            acc[...] += _mm(wm, y_ref[0, 0, i * ts:(i + 1) * ts, :])

    @pl.when(e == N_EXPERTS - 1)
    def _():
        y = DEEPNORM_ALPHA * x_ref[...] + m5_ref[0] * acc[...]
        o_ref[...] = _layer_stats(y) * g_ref[...] + b_ref[...]


def _combine(ys, code_t, gate_t, cum_flat, x, modrows, g, b, *, seg_rows, tile, row0, row_stride):
    n, d = x.shape
    nseg = n // seg_rows
    cap = ys.shape[2]
    tj = tile
    nt = seg_rows // tj
    ts = min(SLOT_TILE, cap)
    return pl.pallas_call(
        functools.partial(_combine_kernel, cap=cap, n_tiles=nt, ts=ts),
        grid_spec=pltpu.PrefetchScalarGridSpec(
            num_scalar_prefetch=1,
            grid=(nseg, nt, N_EXPERTS),
            in_specs=[pl.BlockSpec((1, 1, cap, d), lambda s, j, e, *_: (e, s, 0, 0)),
                      pl.BlockSpec((1, tj, N_EXPERTS), lambda s, j, e, *_: (s, j, 0)),
                      pl.BlockSpec((1, tj, N_EXPERTS), lambda s, j, e, *_: (s, j, 0)),
                      pl.BlockSpec((tj, d), lambda s, j, e, *_: (s * nt + j, 0)),
                      pl.BlockSpec((1, 1, d), lambda s, j, e, *_: ((row0 + s * row_stride) * 6 + 5, 0, 0)),
                      pl.BlockSpec((1, d), lambda s, j, e, *_: (0, 0)),
                      pl.BlockSpec((1, d), lambda s, j, e, *_: (0, 0))],
            out_specs=pl.BlockSpec((tj, d), lambda s, j, e, *_: (s * nt + j, 0)),
            scratch_shapes=[pltpu.VMEM((tj, d), F32)]),
        out_shape=jax.ShapeDtypeStruct((n, d), F32),
        compiler_params=_cparams(("arbitrary", "arbitrary", "arbitrary")),
        name="moe_combine",
    )(cum_flat, ys, code_t, gate_t, x, modrows, g.reshape(1, d), b.reshape(1, d))


def _moe(x, modrows, w_router, w1, w3, w2, layer, g, b, *, seg_rows, row0, row_stride):
    n, d = x.shape
    nseg = n // seg_rows
    u, code, gate, cum = _router(x, modrows, w_router, seg_rows=seg_rows, row0=row0, row_stride=row_stride)
    tile = code.shape[3]
    code_e = code.transpose(0, 2, 1, 3)
    code_t = code.transpose(0, 1, 3, 2).reshape(nseg, seg_rows, N_EXPERTS)
    gate_t = gate.transpose(0, 1, 3, 2).reshape(nseg, seg_rows, N_EXPERTS)
    cum_e = cum[..., 0].transpose(0, 2, 1)
    cum_flat = jnp.pad(cum_e, ((0, 0), (0, 0), (1, 0))).reshape(-1)
    xs = _gather(u, code_e, cum_flat, seg_rows=seg_rows)
    cap = xs.shape[2]
    ys = _ffn(xs.reshape(N_EXPERTS, nseg * cap, d), w1, w3, w2, layer).reshape(N_EXPERTS, nseg, cap, d)
    return _combine(ys, code_t, gate_t, cum_flat, x, modrows, g, b, seg_rows=seg_rows, tile=tile,
                    row0=row0, row_stride=row_stride)


def kernel(x, c, ctx, c_ctx, w_mod, b_mod, w_in, w_out, ln_g, ln_b, gdn_conv, gdn_a_log, gdn_dt_bias,
           gdn_norm, gla_w_gate, gla_b_gate, gla_norm, hgrn_gamma, hgrn_norm, na_rpb,
           moe_router, moe_w1, moe_w3, moe_w2):
    batch, seq, d = x.shape
    n_ctx = ctx.shape[1]
    xl = x.reshape(batch * seq, d)
    xc = ctx.reshape(batch * n_ctx, d)

    cc = jnp.zeros((8, d), F32).at[:batch].set(c).at[batch].set(c_ctx)
    mods = _modulation(cc, w_mod, b_mod)

    w_in_r = _regroup_w_in(w_in)
    w_out_b = w_out.astype(BF16)
    w1_b, w3_b, w2_b = moe_w1.astype(BF16), moe_w3.astype(BF16), moe_w2.astype(BF16)
    low = jnp.asarray(_LOW_MATS, BF16)

    for l in range(DEPTH):
        keep_ctx = l < DEPTH - 1
        modrows = mods[l].reshape(8 * 6, 1, d)
        p_lat = _inproj(xl, modrows, w_in_r, l, seg_rows=seq, row0=0)
        p_ctx = _inproj(xc, modrows, w_in_r, l, seg_rows=batch * n_ctx, row0=batch)
        mixes = [
            _gdn(p_lat, p_ctx, gdn_conv[l], gdn_a_log[l], gdn_dt_bias[l], gdn_norm[l], batch=batch),
            _gla(p_lat, p_ctx, gla_w_gate[l], gla_b_gate[l], gla_norm[l], low, batch=batch),
            _hgrn(p_lat, p_ctx, hgrn_gamma, hgrn_norm[l], low, batch=batch, layer=l),
            _na(p_lat, p_ctx, na_rpb[l], batch=batch),
        ]
        xl = _outproj([m[0] for m in mixes], w_out_b, l, xl, modrows, ln_g[l, 0], ln_b[l, 0],
                      seg_rows=seq, row0=0)
        if keep_ctx:
            xc = _outproj([m[1] for m in mixes], w_out_b, l, xc, modrows, ln_g[l, 0], ln_b[l, 0],
                          seg_rows=batch * n_ctx, row0=batch)
        moe_args = (moe_router[l], w1_b, w3_b, w2_b, l, ln_g[l, 1], ln_b[l, 1])
        xl = _moe(xl, modrows, *moe_args, seg_rows=seq, row0=0, row_stride=1)
        if keep_ctx:
            xc = _moe(xc, modrows, *moe_args, seg_rows=n_ctx, row0=batch, row_stride=0)
    return xl.reshape(batch, seq, d)
```

```python
import functools

import numpy as np
import jax
import jax.numpy as jnp
from jax import lax
from jax.experimental import pallas as pl
from jax.experimental.pallas import tpu as pltpu

F32 = jnp.float32
BF16 = jnp.bfloat16
I32 = jnp.int32

D_MODEL = 2048
DEPTH = 4
GRID_W = 64
HEAD_DIM = 128
N_GROUPS = 4
GROUP_WIDTH = D_MODEL // N_GROUPS
GROUP_HEADS = GROUP_WIDTH // HEAD_DIM
GDN_CONV = 5
GDN_CHUNK = 64
ROWS = 256
GLA_DK = HEAD_DIM // 2
GLA_RANK = 16
GLA_TAU = 16.0
NA_ROWS = 8
NA_COLS = 16
ROPE_BASE = 10000.0
N_EXPERTS = 16
EC_CAPACITY = 2
D_EXPERT = D_MODEL // 2
DEEPNORM_ALPHA = (2 * DEPTH) ** 0.25
LN_EPS = 1e-5
NORM_EPS = 1e-6
NEG_BIG = -1e30

ROUTE_TILE = 256
LANES = 128
SUBLANES = 8
VMEM_LIMIT = 56 * 1024 * 1024

_OFF = {}
_o = 0
for _name, _w in (("a_q", 512), ("a_k", 512), ("a_v", 512), ("a_z", 512), ("a_beta", 8), ("a_dec", 8),
                  ("b_q", 256), ("b_k", 256), ("b_v", 512), ("b_r", 512), ("b_code", 32),
                  ("c_q", 512), ("c_i", 512), ("c_g", 512), ("c_f", 1024),
                  ("d_q", 512), ("d_k", 512), ("d_v", 512)):
    _OFF[_name] = _o
    _o += _w
N_IN = _o

BLK = dict(a_q=0, a_k=4, a_v=8, a_z=12, c_q=16, c_i=20, c_g=24, c_f0=28, c_f1=32,
           d_q=36, d_k=40, d_v=44, b_v=48, b_r=52, b_q=56, b_k=60, small=64)
NP_BLOCKS = 66
NP = NP_BLOCKS * LANES
INPROJ_TN = NP // 3
SM_BETA, SM_DEC, SM_CODE = 0, 8, 16


def _regroup_w_in(w_in, dtype=BF16):
    def cols(a, b):
        return w_in[..., a:b]

    def heads_padded(name):
        w = cols(_OFF[name], _OFF[name] + GROUP_HEADS * GLA_DK)
        w = w.reshape(w.shape[:-1] + (GROUP_HEADS, GLA_DK))
        w = jnp.pad(w, [(0, 0)] * (w.ndim - 1) + [(0, LANES - GLA_DK)])
        return w.reshape(w.shape[:-2] + (GROUP_HEADS * LANES,))

    small = jnp.concatenate([cols(_OFF["a_beta"], _OFF["a_beta"] + 16), cols(_OFF["b_code"], _OFF["b_code"] + 32)],
                            axis=-1)
    small = jnp.pad(small, [(0, 0)] * (small.ndim - 1) + [(0, (NP_BLOCKS - BLK["small"]) * LANES - 48)])
    parts = [cols(_OFF["a_q"], _OFF["a_q"] + 2048),
             cols(_OFF["c_q"], _OFF["c_q"] + 2560),
             cols(_OFF["d_q"], _OFF["d_q"] + 1536),
             cols(_OFF["b_v"], _OFF["b_v"] + 1024),
             heads_padded("b_q"), heads_padded("b_k"), small]
    return jnp.concatenate(parts, axis=-1).astype(dtype)


def _cparams(sem):
    return pltpu.CompilerParams(dimension_semantics=sem, vmem_limit_bytes=VMEM_LIMIT)


def _sigmoid(x):
    return 1.0 / (1.0 + jnp.exp(-x))


def _silu(x):
    return x * _sigmoid(x)


def _split2(x):
    hi = x.astype(BF16)
    lo = (x - hi.astype(F32)).astype(BF16)
    return hi, lo


def _split3(x):
    hi = x.astype(BF16)
    r = x - hi.astype(F32)
    mid = r.astype(BF16)
    lo = (r - mid.astype(F32)).astype(BF16)
    return hi, mid, lo


def _mm(a, b):
    return jnp.dot(a, b, preferred_element_type=F32)


def _mm_nt(a, b):
    return lax.dot_general(a, b, (((1,), (1,)), ((), ())), preferred_element_type=F32)


def _mm_tn(a, b):
    return lax.dot_general(a, b, (((0,), (0,)), ((), ())), preferred_element_type=F32)


def _mm_x2(a, b):
    ah, al = _split2(a)
    bh, bl = _split2(b)
    return _mm(ah, bh) + _mm(ah, bl) + _mm(al, bh)


def _mm_exact_lhs(a_bf, b):
    bh, bm, bl = _split3(b)
    return _mm(a_bf, bh) + _mm(a_bf, bm) + _mm(a_bf, bl)


def _mm_exact_rhs(a, b_bf):
    ah, am, al = _split3(a)
    return _mm(ah, b_bf) + _mm(am, b_bf) + _mm(al, b_bf)


def _layer_stats(x):
    mu = jnp.mean(x, axis=-1, keepdims=True)
    xc = x - mu
    var = jnp.mean(xc * xc, axis=-1, keepdims=True)
    return xc * lax.rsqrt(var + LN_EPS)


def _mod_kernel(c_ref, w_ref, b_ref, o_ref):
    s = _silu(c_ref[...])
    hi, lo = _split2(s)
    w = w_ref[0].astype(BF16)
    o_ref[0] = _mm(hi, w) + _mm(lo, w) + b_ref[0]


def _modulation(cc, w_mod, b_mod):
    depth, d, n6 = w_mod.shape
    tn = 1024
    return pl.pallas_call(
        _mod_kernel,
        grid=(depth, n6 // tn),
        in_specs=[pl.BlockSpec((8, d), lambda l, j: (0, 0)),
                  pl.BlockSpec((1, d, tn), lambda l, j: (l, 0, j)),
                  pl.BlockSpec((1, 1, tn), lambda l, j: (l, 0, j))],
        out_specs=pl.BlockSpec((1, 8, tn), lambda l, j: (l, 0, j)),
        out_shape=jax.ShapeDtypeStruct((depth, 8, n6), F32),
        compiler_params=_cparams(("arbitrary", "arbitrary")),
        name="modulation",
    )(cc, w_mod, b_mod.reshape(depth, 1, n6))


def _inproj_kernel(x_ref, sh_ref, sc_ref, w_ref, o_ref, u_scr):
    @pl.when(pl.program_id(1) == 0)
    def _():
        y = _layer_stats(x_ref[...]) * (1.0 + sc_ref[0]) + sh_ref[0]
        u_scr[...] = y.astype(BF16)

    o_ref[...] = _mm(u_scr[...], w_ref[0])


def _inproj(x, modrows, w, layer, *, seg_rows, row0):
    n, d = x.shape
    tm = min(512, n, seg_rows)
    tn = INPROJ_TN
    tiles_per_seg = seg_rows // tm

    def mrow(k):
        return lambda i, j: ((row0 + i // tiles_per_seg) * 6 + k, 0, 0)

    return pl.pallas_call(
        _inproj_kernel,
        grid=(n // tm, NP // tn),
        in_specs=[pl.BlockSpec((tm, d), lambda i, j: (i, 0)),
                  pl.BlockSpec((1, 1, d), mrow(0)),
                  pl.BlockSpec((1, 1, d), mrow(1)),
                  pl.BlockSpec((1, d, tn), lambda i, j: (layer, 0, j))],
        out_specs=pl.BlockSpec((tm, tn), lambda i, j: (i, j)),
        out_shape=jax.ShapeDtypeStruct((n, NP), F32),
        scratch_shapes=[pltpu.VMEM((tm, d), BF16)],
        compiler_params=_cparams(("arbitrary", "arbitrary")),
        name="inproj",
    )(x, modrows, modrows, w)


def _outproj_kernel(m0, m1, m2, m3, w_ref, x_ref, gate_ref, g_ref, b_ref, o_ref):
    gw = GROUP_WIDTH
    acc = _mm(m0[...], w_ref[0, 0 * gw:1 * gw, :])
    acc += _mm(m1[...], w_ref[0, 1 * gw:2 * gw, :])
    acc += _mm(m2[...], w_ref[0, 2 * gw:3 * gw, :])
    acc += _mm(m3[...], w_ref[0, 3 * gw:4 * gw, :])
    y = DEEPNORM_ALPHA * x_ref[...] + gate_ref[0] * acc
    o_ref[...] = _layer_stats(y) * g_ref[...] + b_ref[...]


def _outproj(mixes, w, layer, x, modrows, g, b, *, seg_rows, row0):
    n, d = x.shape
    tm = min(512, n, seg_rows)
    tiles_per_seg = seg_rows // tm
    mspec = pl.BlockSpec((tm, GROUP_WIDTH), lambda i: (i, 0))
    return pl.pallas_call(
        _outproj_kernel,
        grid=(n // tm,),
        in_specs=[mspec, mspec, mspec, mspec,
                  pl.BlockSpec((1, d, d), lambda i: (layer, 0, 0)),
                  pl.BlockSpec((tm, d), lambda i: (i, 0)),
                  pl.BlockSpec((1, 1, d), lambda i: ((row0 + i // tiles_per_seg) * 6 + 2, 0, 0)),
                  pl.BlockSpec((1, d), lambda i: (0, 0)),
                  pl.BlockSpec((1, d), lambda i: (0, 0))],
        out_specs=pl.BlockSpec((tm, d), lambda i: (i, 0)),
        out_shape=jax.ShapeDtypeStruct((n, d), F32),
        compiler_params=_cparams(("arbitrary",)),
        name="outproj",
    )(*mixes, w, x, modrows, g.reshape(1, d), b.reshape(1, d))


def _bwd_chunk_index(i, n_ctx_chunks, n_chunks):
    return jnp.where(i < n_ctx_chunks, n_ctx_chunks - 1 - i, n_chunks + n_ctx_chunks - 1 - i)


def _rms_gate_store(of_s, ob_s, g_ref, gate_fn, gate_refs, out_refs, seg_bounds):
    for (lo, hi), gate_ref, out_ref in zip(seg_bounds, gate_refs, out_refs):
        n = hi - lo
        tile = min(512, n)
        for r0 in range(0, n, tile):
            o = of_s[lo + r0: lo + r0 + tile, :] + ob_s[lo + r0: lo + r0 + tile, :]
            y = o * lax.rsqrt(jnp.mean(o * o, axis=-1, keepdims=True) + NORM_EPS) * g_ref[...]
            out_ref[r0:r0 + tile, :] = (y * gate_fn(gate_ref[r0:r0 + tile, :])).astype(out_ref.dtype)


def _select_col(x, lane, c):
    return jnp.sum(jnp.where(lane == c, x, 0.0), axis=1, keepdims=True)


def _rows8(row):
    return jnp.broadcast_to(row, (SUBLANES, LANES))


def _gdn_kernel(ql, kl, vl, zl, sl, qc, kc, vc, zc, sc, wq, wk, wv, alog, dtb, gn,
                ol, oc, q_s, k_s, v_s, tok_s, pad_s, u_s, w_s, qg_s, kd_s, p_s, ge_s, *, n_ctx, n_lat):
    h = pl.program_id(1)
    n_tot = n_ctx + n_lat
    segs = ((0, n_ctx), (n_ctx, n_tot))

    def conv_into(x_ref, w_ref, dst, lo, n, l2_scale):
        pad_s[0:8, :] = jnp.zeros((8, LANES), F32)
        pad_s[8:8 + n, :] = x_ref[...]
        pad_s[8 + n:16 + n, :] = jnp.zeros((8, LANES), F32)
        tile = min(512, n)
        for r0 in range(0, n, tile):
            acc = jnp.zeros((tile, LANES), F32)
            for i in range(GDN_CONV):
                s0 = 8 + r0 + i - GDN_CONV // 2
                acc = acc + pad_s[s0:s0 + tile, :] * w_ref[i:i + 1, :]
            y = _silu(acc)
            if l2_scale is not None:
                y = y * (lax.rsqrt(jnp.sum(y * y, axis=-1, keepdims=True) + NORM_EPS) * l2_scale)
            dst[lo + r0: lo + r0 + tile, :] = y

    for (lo, hi), (xq, xk, xv) in zip(segs, ((qc, kc, vc), (ql, kl, vl))):
        conv_into(xq, wq, q_s, lo, hi - lo, HEAD_DIM ** -0.5)
        conv_into(xk, wk, k_s, lo, hi - lo, 1.0)
        conv_into(xv, wv, v_s, lo, hi - lo, None)

    def tok_into(s_ref, lo, n):
        tile = min(512, n)
        lane = lax.broadcasted_iota(I32, (tile, LANES), 1)
        for r0 in range(0, n, tile):
            sm = s_ref[r0:r0 + tile, :]
            cols = []
            for d in range(2):
                beta = _sigmoid(_select_col(sm, lane, SM_BETA + d * GROUP_HEADS + h))
                dec = _select_col(sm, lane, SM_DEC + d * GROUP_HEADS + h)
                a_neg = -jnp.exp(_select_col(alog[...], lane[0:1], d * GROUP_HEADS + h))
                bias = _select_col(dtb[...], lane[0:1], d * GROUP_HEADS + h)
                la = a_neg * jax.nn.softplus(dec + bias)
                cols += [beta, la]
            t = jnp.where(lane == 0, cols[0], jnp.where(lane == 1, cols[1],
                          jnp.where(lane == 2, cols[2], jnp.where(lane == 3, cols[3], 0.0))))
            tok_s[lo + r0: lo + r0 + tile, :] = t

    tok_into(sc, 0, n_ctx)
    tok_into(sl, n_ctx, n_lat)

    c = GDN_CHUNK
    nb = ROWS // c
    r = lax.broadcasted_iota(I32, (ROWS, ROWS), 0)
    cc = lax.broadcasted_iota(I32, (ROWS, ROWS), 1)
    same = (r // c) == (cc // c)
    same_bf = same.astype(BF16)
    eye = r == cc
    eye_f = eye.astype(F32)

    def phase1(si, _):
        off = pl.multiple_of(si * ROWS, ROWS)
        q = q_s[pl.ds(off, ROWS), :]
        k = k_s[pl.ds(off, ROWS), :]
        v = v_s[pl.ds(off, ROWS), :]
        tk = tok_s[pl.ds(off, ROWS), :]
        qb = q.astype(BF16)
        kb = k.astype(BF16)
        kk = _mm_nt(kb, kb)
        qk = _mm_nt(qb, kb)
        dirs = (0, 1)
        beta = [tk[:, 2 * d:2 * d + 1] for d in dirs]
        incl = [same & ((cc >= r) if d else (cc <= r)) for d in dirs]
        strict = [same & ((cc > r) if d else (cc < r)) for d in dirs]
        gi_full = [_mm_exact_lhs(incl[d].astype(BF16), jnp.broadcast_to(tk[:, 2 * d + 1:2 * d + 2], (ROWS, LANES)))
                   for d in dirs]
        gi = [jnp.concatenate([gi_full[d], gi_full[d]], axis=1) for d in dirs]
        gj = [_mm_exact_lhs(same_bf, jnp.where(eye, gi[d], 0.0)) for d in dirs]
        diff = [gi[d] - gj[d] for d in dirs]
        a = [beta[d] * kk * jnp.exp(jnp.where(strict[d], diff[d], -jnp.inf)) for d in dirs]
        t = [eye_f - a[d] for d in dirs]
        pb = [a[d].astype(BF16) for d in dirs]
        for _ in range(5):
            pb = [_mm(pb[d], pb[d]).astype(BF16) for d in dirs]
            t = [t[d] + _mm(t[d].astype(BF16), pb[d]) for d in dirs]
        eg = [jnp.exp(gi_full[d]) for d in dirs]
        sol = [_mm_x2(t[d], jnp.concatenate([beta[d] * v, (beta[d] * eg[d]) * k], axis=1)) for d in dirs]
        for d in dirs:
            p = qk * jnp.exp(jnp.where(incl[d], diff[d], -jnp.inf))
            for kb_ in range(nb):
                p_s[d, pl.ds(pl.multiple_of(off + kb_ * c, c), c), :] = (
                    p[kb_ * c:(kb_ + 1) * c, kb_ * c:(kb_ + 1) * c].astype(BF16))
            u_s[d, pl.ds(off, ROWS), :] = sol[d][:, :LANES]
            w_s[d, pl.ds(off, ROWS), :] = sol[d][:, LANES:].astype(BF16)
            gr = gi_full[d].reshape(nb, c, LANES)
            gl = gr[:, 0:1, :] if d else gr[:, c - 1:c, :]
            g_last = jnp.broadcast_to(gl, (nb, c, LANES)).reshape(ROWS, LANES)
            qg_s[d, pl.ds(off, ROWS), :] = (q * eg[d]).astype(BF16)
            kd_s[d, pl.ds(off, ROWS), :] = (k * jnp.exp(g_last - gi_full[d])).astype(BF16)
            ge = jnp.broadcast_to(jnp.exp(gl), (nb, SUBLANES, LANES)).reshape(nb * SUBLANES, LANES)
            ge_s[d, pl.ds(pl.multiple_of(si * nb * SUBLANES, nb * SUBLANES), nb * SUBLANES), :] = ge
        return 0

    lax.fori_loop(0, n_tot // ROWS, phase1, 0)

    of_s, ob_s = q_s, k_s
    n_cc = n_ctx // c
    n_ch = n_tot // c

    def step(i, carry):
        dirs = (0, 1)
        chs = (i, _bwd_chunk_index(i, n_cc, n_ch))
        offs = [pl.multiple_of(chs[d] * c, c) for d in dirs]
        ge = [ge_s[d, pl.ds(pl.multiple_of(chs[d] * SUBLANES, SUBLANES), SUBLANES), :][0:1, :] for d in dirs]
        wq = [jnp.concatenate([w_s[d, pl.ds(offs[d], c), :], qg_s[d, pl.ds(offs[d], c), :]], axis=0) for d in dirs]
        r1 = [_mm(wq[d], carry[d].astype(BF16)) for d in dirs]
        vb = [(u_s[d, pl.ds(offs[d], c), :] - r1[d][:c]).astype(BF16) for d in dirs]
        upd = [_mm_tn(kd_s[d, pl.ds(offs[d], c), :], vb[d]) for d in dirs]
        o = [r1[d][c:] + _mm(p_s[d, pl.ds(offs[d], c), :], vb[d]) for d in dirs]
        of_s[pl.ds(offs[0], c), :] = o[0]
        ob_s[pl.ds(offs[1], c), :] = o[1]
        return tuple(ge[d] * carry[d] + upd[d] for d in dirs)

    z0 = jnp.zeros((HEAD_DIM, HEAD_DIM), F32)
    lax.fori_loop(0, n_ch, step, (z0, z0))

    _rms_gate_store(of_s, ob_s, gn, _silu, (zc, zl), (oc, ol), segs)


def _gdn(p_lat, p_ctx, conv_w, a_log, dt_bias, gnorm, *, batch):
    n_lat = p_lat.shape[0] // batch
    n_ctx = p_ctx.shape[0] // batch
    n_tot = n_lat + n_ctx
    assert n_lat % ROWS == 0 and n_ctx % ROWS == 0
    n_ch = n_tot // GDN_CHUNK

    def blk(n, base):
        return pl.BlockSpec((n, LANES), lambda b, h: (b, base + h))

    def small(n):
        return pl.BlockSpec((n, LANES), lambda b, h: (b, BLK["small"]))

    def cw(base):
        return pl.BlockSpec((8, LANES), lambda b, h: (0, base + h))

    row = pl.BlockSpec((1, LANES), lambda b, h: (0, 0))
    conv_p = jnp.zeros((8, 3 * GROUP_WIDTH), F32).at[:GDN_CONV].set(conv_w)
    alog_row = jnp.zeros((1, LANES), F32).at[0, :2 * GROUP_HEADS].set(a_log.reshape(-1))
    dtb_row = jnp.zeros((1, LANES), F32).at[0, :2 * GROUP_HEADS].set(dt_bias.reshape(-1))
    lat_in = [blk(n_lat, BLK[k]) for k in ("a_q", "a_k", "a_v", "a_z")] + [small(n_lat)]
    ctx_in = [blk(n_ctx, BLK[k]) for k in ("a_q", "a_k", "a_v", "a_z")] + [small(n_ctx)]
    kern = functools.partial(_gdn_kernel, n_ctx=n_ctx, n_lat=n_lat)
    return pl.pallas_call(
        kern,
        grid=(batch, GROUP_HEADS),
        in_specs=lat_in + ctx_in + [cw(0), cw(4), cw(8), row, row, row],
        out_specs=[pl.BlockSpec((n_lat, LANES), lambda b, h: (b, h)),
                   pl.BlockSpec((n_ctx, LANES), lambda b, h: (b, h))],
        out_shape=[jax.ShapeDtypeStruct((batch * n_lat, GROUP_WIDTH), BF16),
                   jax.ShapeDtypeStruct((batch * n_ctx, GROUP_WIDTH), BF16)],
        scratch_shapes=[pltpu.VMEM((n_tot, LANES), F32)] * 4
        + [pltpu.VMEM((n_lat + 16, LANES), F32),
           pltpu.VMEM((2, n_tot, LANES), F32),
           pltpu.VMEM((2, n_tot, LANES), BF16),
           pltpu.VMEM((2, n_tot, LANES), BF16),
           pltpu.VMEM((2, n_tot, LANES), BF16),
           pltpu.VMEM((2, n_tot, GDN_CHUNK), BF16),
           pltpu.VMEM((2, n_ch * SUBLANES, LANES), F32)],
        compiler_params=_cparams(("arbitrary", "arbitrary")),
        name="gdn",
    )(*([p_lat] * 5), *([p_ctx] * 5), conv_p, conv_p, conv_p, alog_row, dtb_row, gnorm.reshape(1, LANES))


DROWS = 128
N_LEVELS = 7


def _cumsum_matrices():
    i = np.arange(DROWS)[:, None]
    u = np.arange(DROWS)[None, :]
    return np.stack([(u <= i), (u >= i)]).astype(np.float32)


_LOW_MATS = _cumsum_matrices()


def _level_index(rev):
    r = lax.broadcasted_iota(I32, (DROWS, DROWS), 0)
    cc = lax.broadcasted_iota(I32, (DROWS, DROWS), 1)
    lev = 31 - lax.clz(r ^ cc)
    earlier = (cc > r) if rev else (cc < r)
    return jnp.where(earlier, lev, jnp.where(r == cc, -1, -2))


def _rows_at(b, rowi, m, shifts, cache):
    c = b.shape[0]
    out = None
    for mval, sh in enumerate(shifts):
        if sh not in cache:
            cache[sh] = b if sh == 0 else pltpu.roll(b, sh % c, 0)
        out = cache[sh] if out is None else jnp.where(m == mval, cache[sh], out)
    src = rowi - sum(jnp.where(m == mval, sh, 0) for mval, sh in enumerate(shifts))
    return jnp.where((src >= 0) & (src < c), out, 0.0)


def _block_sums(b, rev, rowi):
    c = DROWS
    win, wout, rolled = {}, {}, {}
    for lv in (1, 2):
        s = 1 << lv
        m = rowi & (s - 1)
        if rev:
            win[lv] = b - _rows_at(b, rowi, m, [-(s - t) for t in range(s)], rolled)
            wout[lv] = _rows_at(b, rowi, m, list(range(s)), rolled) - b
        else:
            win[lv] = b - _rows_at(b, rowi, m, [t + 1 for t in range(s)], rolled)
            wout[lv] = _rows_at(b, rowi, m, [-(s - 1 - t) for t in range(s)], rolled) - b
    for lv in range(3, N_LEVELS + 1):
        s = 1 << lv
        nblk = c // s
        br = b.reshape(nblk, s, LANES)
        edge = br[:, 0:1, :] if rev else br[:, s - 1:s, :]
        zero = jnp.zeros((1, 1, LANES), F32)
        if nblk == 1:
            before = zero
        elif rev:
            before = jnp.concatenate([edge[1:], zero], axis=0)
        else:
            before = jnp.concatenate([zero, edge[:-1]], axis=0)
        win[lv] = (br - before).reshape(c, LANES)
        wout[lv] = (edge - br).reshape(c, LANES)
    return win, wout


def _diag_phase1(probs, low_ref, levs):
    c = DROWS
    idx = range(len(probs))
    rowi = lax.broadcasted_iota(I32, (c, LANES), 0)
    bs = [_mm_exact_lhs(low_ref[1 if p[4] else 0], p[3]) for p in probs]
    sums = [_block_sums(bs[i], probs[i][4], rowi) for i in idx]
    qb = [p[0].astype(BF16) for p in probs]
    kb = [p[1].astype(BF16) for p in probs]
    lev = [levs[1 if p[4] else 0] for p in probs]
    pm = [jnp.where(lev[i] == -1, _mm_nt(qb[i], kb[i]), 0.0) for i in idx]
    for lv in range(N_LEVELS):
        qs = [(probs[i][0] * jnp.exp(probs[i][3] if lv == 0 else sums[i][0][lv])).astype(BF16) for i in idx]
        ks = [kb[i] if lv == 0 else (probs[i][1] * jnp.exp(sums[i][1][lv])).astype(BF16) for i in idx]
        dots = [_mm_nt(qs[i], ks[i]) for i in idx]
        pm = [jnp.where(lev[i] == lv, dots[i], pm[i]) for i in idx]
    o_intra = [_mm(pm[i].astype(BF16), probs[i][2]) for i in idx]
    qg = [(probs[i][0] * jnp.exp(sums[i][0][N_LEVELS])).astype(BF16) for i in idx]
    kd = [(probs[i][1] * jnp.exp(sums[i][1][N_LEVELS])).astype(BF16) for i in idx]
    a_end = [jnp.exp(bs[i][0:1, :] if probs[i][4] else bs[i][c - 1:c, :]) for i in idx]
    return [(o_intra[i], qg[i], kd[i], a_end[i]) for i in idx]


DIAG_CHUNKS_PER_STEP = 2


def _diag_run_phase1(load, lo, n_rows, low_ref, levs, v_bs, o_s, qg_s, kd_s, ae_s):
    def step(pi, _):
        probs, offs = [], []
        for u in range(DIAG_CHUNKS_PER_STEP):
            r0 = pl.multiple_of((pi * DIAG_CHUNKS_PER_STEP + u) * DROWS, DROWS)
            q, kf, kb, v, laf, lab = load(r0)
            off = pl.multiple_of(lo + r0, DROWS)
            v_bf = v.astype(BF16)
            v_bs[pl.ds(off, DROWS), :] = v_bf
            probs += [(q, kf, v_bf, laf, False), (q, kb, v_bf, lab, True)]
            offs += [(off, 0), (off, 1)]
        for (off, d), (o_intra, qg, kd, a_end) in zip(offs, _diag_phase1(probs, low_ref, levs)):
            o_s[d][pl.ds(off, DROWS), :] = o_intra
            qg_s[d, pl.ds(off, DROWS), :] = qg
            kd_s[d, pl.ds(off, DROWS), :] = kd
            ci = off // DROWS
            ae_s[d, pl.ds(pl.multiple_of(ci * SUBLANES, SUBLANES), SUBLANES), :] = _rows8(a_end)
        return 0

    lax.fori_loop(0, n_rows // (DIAG_CHUNKS_PER_STEP * DROWS), step, 0)


def _diag_phase2(v_bs, o_s, qg_s, kd_s, ae_s, n_ctx, n_tot):
    n_cc = n_ctx // DROWS
    n_ch = n_tot // DROWS

    def step(i, carry):
        dirs = (0, 1)
        chs = (i, _bwd_chunk_index(i, n_cc, n_ch))
        offs = [pl.multiple_of(chs[d] * DROWS, DROWS) for d in dirs]
        a_end = [ae_s[d, pl.ds(pl.multiple_of(chs[d] * SUBLANES, SUBLANES), SUBLANES), :][0:1, :] for d in dirs]
        stb = [carry[d].astype(BF16) for d in dirs]
        inter = [_mm_nt(qg_s[d, pl.ds(offs[d], DROWS), :], stb[d]) for d in dirs]
        upd = [_mm_tn(v_bs[pl.ds(offs[d], DROWS), :], kd_s[d, pl.ds(offs[d], DROWS), :]) for d in dirs]
        for d in dirs:
            o_s[d][pl.ds(offs[d], DROWS), :] += inter[d]
        return tuple(carry[d] * a_end[d] + upd[d] for d in dirs)

    z0 = jnp.zeros((HEAD_DIM, HEAD_DIM), F32)
    lax.fori_loop(0, n_ch, step, (z0, z0))


def _diag_scratch(n_tot):
    n_ch = n_tot // DROWS
    return [pltpu.VMEM((n_tot, LANES), BF16),
            pltpu.VMEM((n_tot, LANES), F32),
            pltpu.VMEM((n_tot, LANES), F32),
            pltpu.VMEM((2, n_tot, LANES), BF16),
            pltpu.VMEM((2, n_tot, LANES), BF16),
            pltpu.VMEM((2, n_ch * SUBLANES, LANES), F32)]


def _gla_kernel(ql, kl, vl, rl, sl, qc, kc, vc, rc, sc, cos_ref, sin_ref, perm_ref, wg_ref, bg_ref, gn,
                low_ref, ol, oc, v_bs, of_s, ob_s, qg_s, kd_s, ae_s, *, n_ctx, n_lat):
    n_tot = n_ctx + n_lat
    segs = ((0, n_ctx), (n_ctx, n_tot))
    levs = (_level_index(False), _level_index(True))

    for (lo, hi), (xq, xk, xv, xs), rope in zip(segs, ((qc, kc, vc, sc), (ql, kl, vl, sl)), (False, True)):
        def load(r0, xq=xq, xk=xk, xv=xv, xs=xs, rope=rope):
            q = xq[pl.ds(r0, DROWS), :]
            k = xk[pl.ds(r0, DROWS), :]
            if rope:
                cs = cos_ref[pl.ds(r0, DROWS), :]
                sn = sin_ref[pl.ds(r0, DROWS), :]
                q = q * cs + _mm_exact_rhs(q, perm_ref[...]) * sn
                k = k * cs + _mm_exact_rhs(k, perm_ref[...]) * sn
            sm = xs[pl.ds(r0, DROWS), :]
            las = [jax.nn.log_sigmoid(_mm_x2(sm, wg_ref[d, 0]) + bg_ref[d, 0]) * (1.0 / GLA_TAU) for d in range(2)]
            return q * (GLA_DK ** -0.5), k, k, xv[pl.ds(r0, DROWS), :], las[0], las[1]

        _diag_run_phase1(load, lo, hi - lo, low_ref, levs, v_bs, (of_s, ob_s), qg_s, kd_s, ae_s)

    _diag_phase2(v_bs, (of_s, ob_s), qg_s, kd_s, ae_s, n_ctx, n_tot)
    _rms_gate_store(of_s, ob_s, gn, _silu, (rc, rl), (oc, ol), segs)


def _rope_tables(n_lat):
    t = np.arange(n_lat)
    nf = GLA_DK // 4
    inv = (ROPE_BASE ** (-jnp.arange(nf, dtype=F32) / nf))
    lane = np.arange(LANES)
    f = lane % nf
    use_col = (lane % GLA_DK) >= GLA_DK // 2
    first = (lane % (2 * nf)) < nf
    real = lane < GLA_DK
    pos = jnp.where(use_col[None, :], (t % GRID_W)[:, None], (t // GRID_W)[:, None]).astype(F32)
    ang = pos * inv[f][None, :]
    cos = jnp.where(real[None, :], jnp.cos(ang), 1.0)
    sin = jnp.where(real[None, :], jnp.where(first[None, :], -jnp.sin(ang), jnp.sin(ang)), 0.0)
    partner = np.where(first, lane + nf, lane - nf)
    perm = np.zeros((LANES, LANES), np.float32)
    perm[partner[real], lane[real]] = 1.0
    return cos.astype(F32), sin.astype(F32), jnp.asarray(perm, BF16)


def _gla(p_lat, p_ctx, w_gate, b_gate, gnorm, low, *, batch):
    n_lat = p_lat.shape[0] // batch
    n_ctx = p_ctx.shape[0] // batch
    n_tot = n_lat + n_ctx
    assert n_lat % DROWS == 0 and n_ctx % DROWS == 0
    cos, sin, perm = _rope_tables(n_lat)
    wg = jnp.zeros((2, GROUP_HEADS, LANES, LANES), F32)
    wsrc = w_gate.reshape(2, GLA_RANK, GROUP_HEADS, GLA_DK).transpose(0, 2, 1, 3)
    for d in range(2):
        wg = wg.at[d, :, SM_CODE + d * GLA_RANK: SM_CODE + (d + 1) * GLA_RANK, :GLA_DK].set(wsrc[d])
    bg = jnp.zeros((2, GROUP_HEADS, 1, LANES), F32).at[:, :, 0, :GLA_DK].set(
        b_gate.reshape(2, GROUP_HEADS, GLA_DK))

    def blk(n, base):
        return pl.BlockSpec((n, LANES), lambda b, h: (b, base + h))

    def small(n):
        return pl.BlockSpec((n, LANES), lambda b, h: (b, BLK["small"]))

    def const2(shape):
        return pl.BlockSpec(shape, lambda b, h: (0, 0))

    names = ("b_q", "b_k", "b_v", "b_r")
    kern = functools.partial(_gla_kernel, n_ctx=n_ctx, n_lat=n_lat)
    return pl.pallas_call(
        kern,
        grid=(batch, GROUP_HEADS),
        in_specs=[blk(n_lat, BLK[k]) for k in names] + [small(n_lat)]
        + [blk(n_ctx, BLK[k]) for k in names] + [small(n_ctx)]
        + [const2((n_lat, LANES)), const2((n_lat, LANES)), const2((LANES, LANES)),
           pl.BlockSpec((2, 1, LANES, LANES), lambda b, h: (0, h, 0, 0)),
           pl.BlockSpec((2, 1, 1, LANES), lambda b, h: (0, h, 0, 0)),
           const2((1, LANES)),
           pl.BlockSpec(low.shape, lambda b, h: (0, 0, 0))],
        out_specs=[pl.BlockSpec((n_lat, LANES), lambda b, h: (b, h)),
                   pl.BlockSpec((n_ctx, LANES), lambda b, h: (b, h))],
        out_shape=[jax.ShapeDtypeStruct((batch * n_lat, GROUP_WIDTH), BF16),
                   jax.ShapeDtypeStruct((batch * n_ctx, GROUP_WIDTH), BF16)],
        scratch_shapes=_diag_scratch(n_tot),
        compiler_params=_cparams(("arbitrary", "arbitrary")),
        name="gla",
    )(*([p_lat] * 5), *([p_ctx] * 5), cos, sin, perm, wg, bg, gnorm.reshape(1, LANES), low)


def _hgrn_kernel(ql, il, gl, f0l, f1l, qc, ic, gc, f0c, f1c, gam_ref, gn, low_ref,
                 ol, oc, v_bs, of_s, ob_s, qg_s, kd_s, ae_s, *, n_ctx, n_lat, layer):
    n_tot = n_ctx + n_lat
    segs = ((0, n_ctx), (n_ctx, n_tot))
    levs = (_level_index(False), _level_index(True))
    lbs = []
    for d in range(2):
        gam = gam_ref[d]
        ex = jnp.exp(gam - jnp.max(gam, axis=0, keepdims=True))
        pr = ex / jnp.sum(ex, axis=0, keepdims=True)
        lb = jnp.zeros((1, LANES), F32)
        for m in range(1, layer + 1):
            lb = lb + pr[m:m + 1, :]
        lbs.append(lb)

    for (lo, hi), (xq, xi, xf0, xf1) in zip(segs, ((qc, ic, f0c, f1c), (ql, il, f0l, f1l))):
        def load(r0, xq=xq, xi=xi, xf0=xf0, xf1=xf1):
            fs = [lbs[d] + (1.0 - lbs[d]) * _sigmoid(xf[pl.ds(r0, DROWS), :]) for d, xf in enumerate((xf0, xf1))]
            return (xq[pl.ds(r0, DROWS), :], 1.0 - fs[0], 1.0 - fs[1], xi[pl.ds(r0, DROWS), :],
                    jnp.log(fs[0]), jnp.log(fs[1]))

        _diag_run_phase1(load, lo, hi - lo, low_ref, levs, v_bs, (of_s, ob_s), qg_s, kd_s, ae_s)

    _diag_phase2(v_bs, (of_s, ob_s), qg_s, kd_s, ae_s, n_ctx, n_tot)
    _rms_gate_store(of_s, ob_s, gn, _sigmoid, (gc, gl), (oc, ol), segs)


def _hgrn(p_lat, p_ctx, gamma, gnorm, low, *, batch, layer):
    n_lat = p_lat.shape[0] // batch
    n_ctx = p_ctx.shape[0] // batch
    n_tot = n_lat + n_ctx
    assert n_lat % DROWS == 0 and n_ctx % DROWS == 0

    def blk(n, base):
        return pl.BlockSpec((n, LANES), lambda b, h: (b, base + h))

    names = ("c_q", "c_i", "c_g", "c_f0", "c_f1")
    kern = functools.partial(_hgrn_kernel, n_ctx=n_ctx, n_lat=n_lat, layer=layer)
    return pl.pallas_call(
        kern,
        grid=(batch, GROUP_HEADS),
        in_specs=[blk(n_lat, BLK[k]) for k in names] + [blk(n_ctx, BLK[k]) for k in names]
        + [pl.BlockSpec((2, gamma.shape[1], LANES), lambda b, h: (0, 0, h)),
           pl.BlockSpec((1, LANES), lambda b, h: (0, 0)),
           pl.BlockSpec(low.shape, lambda b, h: (0, 0, 0))],
        out_specs=[pl.BlockSpec((n_lat, LANES), lambda b, h: (b, h)),
                   pl.BlockSpec((n_ctx, LANES), lambda b, h: (b, h))],
        out_shape=[jax.ShapeDtypeStruct((batch * n_lat, GROUP_WIDTH), BF16),
                   jax.ShapeDtypeStruct((batch * n_ctx, GROUP_WIDTH), BF16)],
        scratch_shapes=_diag_scratch(n_tot),
        compiler_params=_cparams(("arbitrary", "arbitrary")),
        name="hgrn",
    )(*([p_lat] * 5), *([p_ctx] * 5), gamma, gnorm.reshape(1, LANES), low)


NA_GROUP = 4
NA_WIN = NA_ROWS + NA_GROUP


def _na_kernel(ws_ref, pat_ref, ql, kl, vl, qc, kc, vc, bias_ref, ol, oc, *, n_groups, win):
    scale = HEAD_DIM ** -0.5
    kcb = kc[...].astype(BF16)
    vcb = vc[...].astype(BF16)
    w = GRID_W
    gq = NA_GROUP * w

    per_step = 2 if n_groups % 2 == 0 else 1

    def groups(it, _):
        gs = [it * per_step + u for u in range(per_step)]
        us = range(per_step)
        qoff = [pl.multiple_of(g * gq, gq) for g in gs]
        koff = [pl.multiple_of(ws_ref[g] * w, w) for g in gs]
        q = [ql[pl.ds(qoff[u], gq), :].astype(BF16) for u in us]
        kb = [kl[pl.ds(koff[u], win * w), :].astype(BF16) for u in us]
        vb = [vl[pl.ds(koff[u], win * w), :].astype(BF16) for u in us]
        s_loc = [_mm_nt(q[u], kb[u]) * scale + bias_ref[0, pat_ref[gs[u]]] for u in us]
        s_ctx = [_mm_nt(q[u], kcb) * scale for u in us]
        m = [jnp.maximum(jnp.max(s_loc[u], axis=-1, keepdims=True), jnp.max(s_ctx[u], axis=-1, keepdims=True))
             for u in us]
        p_loc = [jnp.exp(s_loc[u] - m[u]) for u in us]
        p_ctx = [jnp.exp(s_ctx[u] - m[u]) for u in us]
        den = [jnp.sum(p_loc[u], axis=-1, keepdims=True) + jnp.sum(p_ctx[u], axis=-1, keepdims=True) for u in us]
        o = [_mm(p_loc[u].astype(BF16), vb[u]) + _mm(p_ctx[u].astype(BF16), vcb) for u in us]
        for u in us:
            ol[pl.ds(qoff[u], gq), :] = (o[u] / den[u]).astype(ol.dtype)
        return 0

    lax.fori_loop(0, n_groups // per_step, groups, 0)

    s = _mm_nt(qc[...].astype(BF16), kcb) * scale
    p = jnp.exp(s - jnp.max(s, axis=-1, keepdims=True))
    o = _mm(p.astype(BF16), vcb) / jnp.sum(p, axis=-1, keepdims=True)
    oc[...] = o.astype(oc.dtype)


def _na_plan(n_rows):
    kr = min(NA_ROWS, n_rows)
    win = min(NA_WIN, n_rows)
    n_groups = n_rows // NA_GROUP
    ws = np.clip(np.arange(n_groups) * NA_GROUP - kr // 2, 0, n_rows - win)
    seen, pats, pat_of = {}, [], []
    for g in range(n_groups):
        rows = g * NA_GROUP + np.arange(NA_GROUP)
        rs = np.clip(rows - kr // 2, 0, n_rows - kr)
        krow = ws[g] + np.arange(win)
        ok = (krow[None, :] >= rs[:, None]) & (krow[None, :] < rs[:, None] + kr)
        dr = np.where(ok, krow[None, :] - rows[:, None] + NA_ROWS - 1, 0)
        key = (ok.tobytes(), dr.tobytes())
        if key not in seen:
            seen[key] = len(pats)
            pats.append((ok, dr))
        pat_of.append(seen[key])
    return ws.astype(np.int32), np.asarray(pat_of, np.int32), pats, win


def _na_bias(rpb, pats, win):
    n_h = rpb.shape[0]
    n_p = len(pats)
    ok = np.stack([p[0] for p in pats])
    dr = np.stack([p[1] for p in pats]).reshape(-1)
    row_sel = np.zeros((dr.size, 2 * NA_ROWS - 1), np.float32)
    row_sel[np.arange(dr.size), dr] = 1.0
    cq = np.arange(GRID_W)
    dc = np.clip(cq[None, :] - cq[:, None], -(NA_COLS - 1), NA_COLS - 1) + NA_COLS - 1
    col_start = np.clip(cq - NA_COLS // 2, 0, GRID_W - NA_COLS)
    col_in = (cq[None, :] >= col_start[:, None]) & (cq[None, :] < col_start[:, None] + NA_COLS)
    col_sel = np.zeros((2 * NA_COLS - 1, GRID_W * GRID_W), np.float32)
    col_sel[dc.reshape(-1), np.arange(GRID_W * GRID_W)] = 1.0
    hp = lax.Precision.HIGHEST
    b = jnp.einsum("xr,hrd->hxd", row_sel, rpb.astype(F32), precision=hp)
    b = jnp.einsum("hxd,dq->hxq", b, col_sel, precision=hp)
    b = b.reshape(n_h, n_p, NA_GROUP, win, GRID_W, GRID_W).transpose(0, 1, 2, 4, 3, 5)
    valid = ok[:, :, None, :, None] & col_in[None, None, :, None, :]
    b = jnp.where(valid[None], b, NEG_BIG)
    return b.reshape(n_h, n_p, NA_GROUP * GRID_W, win * GRID_W)


def _na(p_lat, p_ctx, rpb, *, batch):
    n_lat = p_lat.shape[0] // batch
    n_ctx = p_ctx.shape[0] // batch
    n_rows = n_lat // GRID_W
    assert n_rows % NA_GROUP == 0
    ws, pat_of, pats, win = _na_plan(n_rows)
    bias = _na_bias(rpb, pats, win)

    def blk(n, base):
        return pl.BlockSpec((n, LANES), lambda b, h, *_: (b, base + h))

    names = ("d_q", "d_k", "d_v")
    return pl.pallas_call(
        functools.partial(_na_kernel, n_groups=n_rows // NA_GROUP, win=win),
        grid_spec=pltpu.PrefetchScalarGridSpec(
            num_scalar_prefetch=2,
            grid=(batch, GROUP_HEADS),
            in_specs=[blk(n_lat, BLK[k]) for k in names] + [blk(n_ctx, BLK[k]) for k in names]
            + [pl.BlockSpec((1,) + bias.shape[1:], lambda b, h, *_: (h, 0, 0, 0))],
            out_specs=[pl.BlockSpec((n_lat, LANES), lambda b, h, *_: (b, h)),
                       pl.BlockSpec((n_ctx, LANES), lambda b, h, *_: (b, h))]),
        out_shape=[jax.ShapeDtypeStruct((batch * n_lat, GROUP_WIDTH), BF16),
                   jax.ShapeDtypeStruct((batch * n_ctx, GROUP_WIDTH), BF16)],
        compiler_params=_cparams(("arbitrary", "arbitrary")),
        name="na",
    )(jnp.asarray(ws), jnp.asarray(pat_of), *([p_lat] * 3), *([p_ctx] * 3), bias)


def _router_kernel(x_ref, sh_ref, sc_ref, wr_ref, u_ref, code_ref, gate_ref, cum_ref, lg_scr,
                   *, cap, n_tiles, tile):
    j = pl.program_id(1)
    u = _layer_stats(x_ref[...]) * (1.0 + sc_ref[0]) + sh_ref[0]
    u_ref[...] = u.astype(BF16)
    lg_scr[j] = lax.dot_general(wr_ref[...], u, (((1,), (1,)), ((), ())),
                                precision=lax.Precision.HIGHEST, preferred_element_type=F32)

    @pl.when(j == n_tiles - 1)
    def _():
        lg = lg_scr[...]
        ex = jnp.exp(lg - jnp.max(lg, axis=1, keepdims=True))
        aff = ex / jnp.sum(ex, axis=1, keepdims=True)
        bits = lax.bitcast_convert_type(aff, I32)

        def count(mask):
            per = jnp.sum(mask.astype(F32), axis=0)
            return jnp.sum(per, axis=1, keepdims=True)

        def bis(it, thr):
            cand = thr | jnp.left_shift(jnp.int32(1), 30 - it)
            ok = count(bits >= cand[None]) >= float(cap)
            return jnp.where(ok, cand, thr)

        thr = lax.fori_loop(0, 31, bis, jnp.zeros((N_EXPERTS, 1), I32))
        gt = bits > thr[None]
        eq = bits == thr[None]
        need = float(cap) - count(gt)
        r = lax.broadcasted_iota(I32, (tile, tile), 0)
        c = lax.broadcasted_iota(I32, (tile, tile), 1)
        upper = (r < c).astype(BF16)
        carry_eq = jnp.zeros((N_EXPERTS, 1), F32)
        carry_sel = jnp.zeros((N_EXPERTS, 1), F32)
        for t in range(n_tiles):
            eq_t = eq[t].astype(BF16)
            pre_eq = _mm(eq_t, upper) + carry_eq
            sel = gt[t] | (eq[t] & (pre_eq < need))
            sel_b = sel.astype(BF16)
            pos = _mm(sel_b, upper) + carry_sel
            code_ref[0, t] = jnp.where(sel, pos.astype(I32), -1)
            gate_ref[0, t] = aff[t]
            carry_eq = carry_eq + jnp.sum(eq_t.astype(F32), axis=1, keepdims=True)
            carry_sel = carry_sel + jnp.sum(sel_b.astype(F32), axis=1, keepdims=True)
            cum_ref[0, t] = jnp.broadcast_to(carry_sel, (N_EXPERTS, LANES)).astype(I32)


def _router(x, modrows, w_router, *, seg_rows, row0, row_stride):
    n, d = x.shape
    nseg = n // seg_rows
    tile = min(ROUTE_TILE, seg_rows)
    n_tiles = seg_rows // tile
    cap = EC_CAPACITY * seg_rows // N_EXPERTS
    kern = functools.partial(_router_kernel, cap=cap, n_tiles=n_tiles, tile=tile)
    return pl.pallas_call(
        kern,
        grid=(nseg, n_tiles),
        in_specs=[pl.BlockSpec((tile, d), lambda s, j: (s * n_tiles + j, 0)),
                  pl.BlockSpec((1, 1, d), lambda s, j: ((row0 + s * row_stride) * 6 + 3, 0, 0)),
                  pl.BlockSpec((1, 1, d), lambda s, j: ((row0 + s * row_stride) * 6 + 4, 0, 0)),
                  pl.BlockSpec((N_EXPERTS, d), lambda s, j: (0, 0))],
        out_specs=[pl.BlockSpec((tile, d), lambda s, j: (s * n_tiles + j, 0)),
                   pl.BlockSpec((1, n_tiles, N_EXPERTS, tile), lambda s, j: (s, 0, 0, 0)),
                   pl.BlockSpec((1, n_tiles, N_EXPERTS, tile), lambda s, j: (s, 0, 0, 0)),
                   pl.BlockSpec((1, n_tiles, N_EXPERTS, LANES), lambda s, j: (s, 0, 0, 0))],
        out_shape=[jax.ShapeDtypeStruct((n, d), BF16),
                   jax.ShapeDtypeStruct((nseg, n_tiles, N_EXPERTS, tile), I32),
                   jax.ShapeDtypeStruct((nseg, n_tiles, N_EXPERTS, tile), F32),
                   jax.ShapeDtypeStruct((nseg, n_tiles, N_EXPERTS, LANES), I32)],
        scratch_shapes=[pltpu.VMEM((n_tiles, N_EXPERTS, tile), F32)],
        compiler_params=_cparams(("arbitrary", "arbitrary")),
        name="router",
    )(x, modrows, modrows, w_router.T)


SLOT_TILE = 128


def _slot_tiles_touched(cum_ref, base, t, ts, n_slot_tiles):
    lo = cum_ref[base + t]
    hi = cum_ref[base + t + 1]
    return [(lo < (i + 1) * ts) & (hi > i * ts) for i in range(n_slot_tiles)]


def _gather_kernel(cum_ref, u_ref, code_ref, o_ref, acc, *, cap, n_tiles, tile, ts):
    base = (pl.program_id(0) * N_EXPERTS + pl.program_id(1)) * (n_tiles + 1)
    acc[...] = jnp.zeros_like(acc)
    slot = lax.broadcasted_iota(I32, (ts, tile), 0)
    for t in range(n_tiles):
        for i, touched in enumerate(_slot_tiles_touched(cum_ref, base, t, ts, cap // ts)):
            @pl.when(touched)
            def _(t=t, i=i):
                oh = (code_ref[0, 0, t:t + 1, :] == slot + i * ts).astype(BF16)
                acc[i * ts:(i + 1) * ts, :] += _mm(oh, u_ref[t * tile:(t + 1) * tile, :])
    o_ref[0, 0] = acc[...].astype(BF16)


def _gather(u, code_e, cum_flat, *, seg_rows):
    n, d = u.shape
    nseg, _, n_tiles, tile = code_e.shape
    cap = EC_CAPACITY * seg_rows // N_EXPERTS
    ts = min(SLOT_TILE, cap)
    return pl.pallas_call(
        functools.partial(_gather_kernel, cap=cap, n_tiles=n_tiles, tile=tile, ts=ts),
        grid_spec=pltpu.PrefetchScalarGridSpec(
            num_scalar_prefetch=1,
            grid=(nseg, N_EXPERTS),
            in_specs=[pl.BlockSpec((seg_rows, d), lambda s, e, *_: (s, 0)),
                      pl.BlockSpec((1, 1, n_tiles, tile), lambda s, e, *_: (s, e, 0, 0))],
            out_specs=pl.BlockSpec((1, 1, cap, d), lambda s, e, *_: (e, s, 0, 0)),
            scratch_shapes=[pltpu.VMEM((cap, d), F32)]),
        out_shape=jax.ShapeDtypeStruct((N_EXPERTS, nseg, cap, d), BF16),
        compiler_params=_cparams(("arbitrary", "arbitrary")),
        name="moe_gather",
    )(cum_flat, u, code_e)


def _ffn_kernel(x_ref, w1_ref, w3_ref, w2_ref, o_ref):
    x = x_ref[0]
    a = _mm(x, w1_ref[0, 0])
    g = _mm(x, w3_ref[0, 0])
    hid = (_silu(a) * g).astype(BF16)
    o_ref[0] = _mm(hid, w2_ref[0, 0]).astype(BF16)


def _ffn(xs, w1, w3, w2, layer):
    e, r, d = xs.shape
    f = w1.shape[3]
    tr = min(512, r)
    return pl.pallas_call(
        _ffn_kernel,
        grid=(e, r // tr),
        in_specs=[pl.BlockSpec((1, tr, d), lambda i, j: (i, j, 0)),
                  pl.BlockSpec((1, 1, d, f), lambda i, j: (layer, i, 0, 0)),
                  pl.BlockSpec((1, 1, d, f), lambda i, j: (layer, i, 0, 0)),
                  pl.BlockSpec((1, 1, f, d), lambda i, j: (layer, i, 0, 0))],
        out_specs=pl.BlockSpec((1, tr, d), lambda i, j: (i, j, 0)),
        out_shape=jax.ShapeDtypeStruct((e, r, d), BF16),
        compiler_params=_cparams(("arbitrary", "arbitrary")),
        name="moe_ffn",
    )(xs, w1, w3, w2)


COMBINE_EXPERTS = 4
BF16_ROWS = 16


def _window_start(cum_ref, s, e, j, n_tiles, cap, wr):
    lo = cum_ref[(s * N_EXPERTS + e) * (n_tiles + 1) + j]
    return jnp.minimum((lo // BF16_ROWS) * BF16_ROWS, cap - wr)


def _combine_kernel(cum_ref, *refs, cap, n_tiles, wr):
    y_refs = refs[:COMBINE_EXPERTS]
    code_ref, gate_ref, x_ref, m5_ref, g_ref, b_ref, o_ref, acc = refs[COMBINE_EXPERTS:]
    s = pl.program_id(0)
    j = pl.program_id(1)
    eg = pl.program_id(2)

    @pl.when(eg == 0)
    def _():
        acc[...] = jnp.zeros_like(acc)

    tj = code_ref.shape[1]
    lane = lax.broadcasted_iota(I32, (tj, N_EXPERTS), 1)
    main = min(wr, ROUTE_TILE)
    for k, y_ref in enumerate(y_refs):
        e = eg * COMBINE_EXPERTS + k
        base = (s * N_EXPERTS + e) * (n_tiles + 1)
        lo = cum_ref[base + j]
        hi = cum_ref[base + j + 1]
        a = _window_start(cum_ref, s, e, j, n_tiles, cap, wr)

        @pl.when(hi > lo)
        def _(e=e, y_ref=y_ref, a=a, hi=hi):
            code = jnp.sum(jnp.where(lane == e, code_ref[0], 0), axis=1, keepdims=True)
            gate = jnp.sum(jnp.where(lane == e, gate_ref[0], 0.0), axis=1, keepdims=True)
            slot = a + lax.broadcasted_iota(I32, (tj, main), 1)
            acc[...] += _mm(jnp.where(code == slot, gate, 0.0).astype(BF16), y_ref[0:main, :])
            if wr > main:
                @pl.when(hi > a + main)
                def _():
                    slot2 = a + main + lax.broadcasted_iota(I32, (tj, wr - main), 1)
                    acc[...] += _mm(jnp.where(code == slot2, gate, 0.0).astype(BF16), y_ref[main:wr, :])

    @pl.when(eg == N_EXPERTS // COMBINE_EXPERTS - 1)
    def _():
        y = DEEPNORM_ALPHA * x_ref[...] + m5_ref[0] * acc[...]
        o_ref[...] = _layer_stats(y) * g_ref[...] + b_ref[...]


def _combine(ys, code_t, gate_t, cum_flat, x, modrows, g, b, *, seg_rows, tile, row0, row_stride):
    n, d = x.shape
    nseg = n // seg_rows
    cap = ys.shape[2]
    tj = tile
    nt = seg_rows // tj
    wr = min(tj + BF16_ROWS, cap)

    def window(k):
        def index(s, j, eg, cum_ref):
            e = eg * COMBINE_EXPERTS + k
            a = _window_start(cum_ref, s, e, j, nt, cap, wr)
            return pl.multiple_of((e * nseg + s) * cap + a, BF16_ROWS), 0
        return pl.BlockSpec((pl.Element(wr), pl.Element(d)), index)

    return pl.pallas_call(
        functools.partial(_combine_kernel, cap=cap, n_tiles=nt, wr=wr),
        grid_spec=pltpu.PrefetchScalarGridSpec(
            num_scalar_prefetch=1,
            grid=(nseg, nt, N_EXPERTS // COMBINE_EXPERTS),
            in_specs=[window(k) for k in range(COMBINE_EXPERTS)]
            + [pl.BlockSpec((1, tj, N_EXPERTS), lambda s, j, e, *_: (s, j, 0)),
               pl.BlockSpec((1, tj, N_EXPERTS), lambda s, j, e, *_: (s, j, 0)),
               pl.BlockSpec((tj, d), lambda s, j, e, *_: (s * nt + j, 0)),
               pl.BlockSpec((1, 1, d), lambda s, j, e, *_: ((row0 + s * row_stride) * 6 + 5, 0, 0)),
               pl.BlockSpec((1, d), lambda s, j, e, *_: (0, 0)),
               pl.BlockSpec((1, d), lambda s, j, e, *_: (0, 0))],
            out_specs=pl.BlockSpec((tj, d), lambda s, j, e, *_: (s * nt + j, 0)),
            scratch_shapes=[pltpu.VMEM((tj, d), F32)]),
        out_shape=jax.ShapeDtypeStruct((n, d), F32),
        compiler_params=_cparams(("arbitrary", "arbitrary", "arbitrary")),
        name="moe_combine",
    )(cum_flat, *([ys.reshape(N_EXPERTS * nseg * cap, d)] * COMBINE_EXPERTS), code_t, gate_t, x, modrows,
      g.reshape(1, d), b.reshape(1, d))


def _moe(x, modrows, w_router, w1, w3, w2, layer, g, b, *, seg_rows, row0, row_stride):
    n, d = x.shape
    nseg = n // seg_rows
    u, code, gate, cum = _router(x, modrows, w_router, seg_rows=seg_rows, row0=row0, row_stride=row_stride)
    tile = code.shape[3]
    code_e = code.transpose(0, 2, 1, 3)
    code_t = code.transpose(0, 1, 3, 2).reshape(nseg, seg_rows, N_EXPERTS)
    gate_t = gate.transpose(0, 1, 3, 2).reshape(nseg, seg_rows, N_EXPERTS)
    cum_e = cum[..., 0].transpose(0, 2, 1)
    cum_flat = jnp.pad(cum_e, ((0, 0), (0, 0), (1, 0))).reshape(-1)
    xs = _gather(u, code_e, cum_flat, seg_rows=seg_rows)
    cap = xs.shape[2]
    ys = _ffn(xs.reshape(N_EXPERTS, nseg * cap, d), w1, w3, w2, layer).reshape(N_EXPERTS, nseg, cap, d)
    return _combine(ys, code_t, gate_t, cum_flat, x, modrows, g, b, seg_rows=seg_rows, tile=tile,
                    row0=row0, row_stride=row_stride)


def kernel(x, c, ctx, c_ctx, w_mod, b_mod, w_in, w_out, ln_g, ln_b, gdn_conv, gdn_a_log, gdn_dt_bias,
           gdn_norm, gla_w_gate, gla_b_gate, gla_norm, hgrn_gamma, hgrn_norm, na_rpb,
           moe_router, moe_w1, moe_w3, moe_w2):
    batch, seq, d = x.shape
    n_ctx = ctx.shape[1]
    xl = x.reshape(batch * seq, d)
    xc = ctx.reshape(batch * n_ctx, d)

    cc = jnp.zeros((8, d), F32).at[:batch].set(c).at[batch].set(c_ctx)
    mods = _modulation(cc, w_mod, b_mod)

    w_in_r = _regroup_w_in(w_in)
    w_out_b = w_out.astype(BF16)
    w1_b, w3_b, w2_b = moe_w1.astype(BF16), moe_w3.astype(BF16), moe_w2.astype(BF16)
    low = jnp.asarray(_LOW_MATS, BF16)

    for l in range(DEPTH):
        keep_ctx = l < DEPTH - 1
        modrows = mods[l].reshape(8 * 6, 1, d)
        p_lat = _inproj(xl, modrows, w_in_r, l, seg_rows=seq, row0=0)
        p_ctx = _inproj(xc, modrows, w_in_r, l, seg_rows=batch * n_ctx, row0=batch)
        mixes = [
            _gdn(p_lat, p_ctx, gdn_conv[l], gdn_a_log[l], gdn_dt_bias[l], gdn_norm[l], batch=batch),
            _gla(p_lat, p_ctx, gla_w_gate[l], gla_b_gate[l], gla_norm[l], low, batch=batch),
            _hgrn(p_lat, p_ctx, hgrn_gamma, hgrn_norm[l], low, batch=batch, layer=l),
            _na(p_lat, p_ctx, na_rpb[l], batch=batch),
        ]
        xl = _outproj([m[0] for m in mixes], w_out_b, l, xl, modrows, ln_g[l, 0], ln_b[l, 0],
                      seg_rows=seq, row0=0)
        if keep_ctx:
            xc = _outproj([m[1] for m in mixes], w_out_b, l, xc, modrows, ln_g[l, 0], ln_b[l, 0],
                          seg_rows=batch * n_ctx, row0=batch)
        moe_args = (moe_router[l], w1_b, w3_b, w2_b, l, ln_g[l, 1], ln_b[l, 1])
        xl = _moe(xl, modrows, *moe_args, seg_rows=seq, row0=0, row_stride=1)
        if keep_ctx:
            xc = _moe(xc, modrows, *moe_args, seg_rows=n_ctx, row0=batch, row_stride=0)
    return xl.reshape(batch, seq, d)
```

```python
import functools

import numpy as np
import jax
import jax.numpy as jnp
from jax import lax
from jax.experimental import pallas as pl
from jax.experimental.pallas import tpu as pltpu

F32 = jnp.float32
BF16 = jnp.bfloat16
I32 = jnp.int32

D_MODEL = 2048
DEPTH = 4
GRID_W = 64
HEAD_DIM = 128
N_GROUPS = 4
GROUP_WIDTH = D_MODEL // N_GROUPS
GROUP_HEADS = GROUP_WIDTH // HEAD_DIM
GDN_CONV = 5
GDN_CHUNK = 64
ROWS = 256
GLA_DK = HEAD_DIM // 2
GLA_RANK = 16
GLA_TAU = 16.0
NA_ROWS = 8
NA_COLS = 16
ROPE_BASE = 10000.0
N_EXPERTS = 16
EC_CAPACITY = 2
D_EXPERT = D_MODEL // 2
DEEPNORM_ALPHA = (2 * DEPTH) ** 0.25
LN_EPS = 1e-5
NORM_EPS = 1e-6
NEG_BIG = -1e30

ROUTE_TILE = 256
LANES = 128
SUBLANES = 8
VMEM_LIMIT = 56 * 1024 * 1024

_OFF = {}
_o = 0
for _name, _w in (("a_q", 512), ("a_k", 512), ("a_v", 512), ("a_z", 512), ("a_beta", 8), ("a_dec", 8),
                  ("b_q", 256), ("b_k", 256), ("b_v", 512), ("b_r", 512), ("b_code", 32),
                  ("c_q", 512), ("c_i", 512), ("c_g", 512), ("c_f", 1024),
                  ("d_q", 512), ("d_k", 512), ("d_v", 512)):
    _OFF[_name] = _o
    _o += _w
N_IN = _o

BLK = dict(a_q=0, a_k=4, a_v=8, a_z=12, c_q=16, c_i=20, c_g=24, c_f0=28, c_f1=32,
           d_q=36, d_k=40, d_v=44, b_v=48, b_r=52, b_q=56, b_k=60, small=64)
NP_BLOCKS = 66
NP = NP_BLOCKS * LANES
INPROJ_TN = NP // 3
SM_BETA, SM_DEC, SM_CODE = 0, 8, 16


def _regroup_w_in(w_in, dtype=BF16):
    def cols(a, b):
        return w_in[..., a:b]

    def heads_padded(name):
        w = cols(_OFF[name], _OFF[name] + GROUP_HEADS * GLA_DK)
        w = w.reshape(w.shape[:-1] + (GROUP_HEADS, GLA_DK))
        w = jnp.pad(w, [(0, 0)] * (w.ndim - 1) + [(0, LANES - GLA_DK)])
        return w.reshape(w.shape[:-2] + (GROUP_HEADS * LANES,))

    small = jnp.concatenate([cols(_OFF["a_beta"], _OFF["a_beta"] + 16), cols(_OFF["b_code"], _OFF["b_code"] + 32)],
                            axis=-1)
    small = jnp.pad(small, [(0, 0)] * (small.ndim - 1) + [(0, (NP_BLOCKS - BLK["small"]) * LANES - 48)])
    parts = [cols(_OFF["a_q"], _OFF["a_q"] + 2048),
             cols(_OFF["c_q"], _OFF["c_q"] + 2560),
             cols(_OFF["d_q"], _OFF["d_q"] + 1536),
             cols(_OFF["b_v"], _OFF["b_v"] + 1024),
             heads_padded("b_q"), heads_padded("b_k"), small]
    return jnp.concatenate(parts, axis=-1).astype(dtype)


def _cparams(sem):
    return pltpu.CompilerParams(dimension_semantics=sem, vmem_limit_bytes=VMEM_LIMIT)


def _sigmoid(x):
    return 1.0 / (1.0 + jnp.exp(-x))


def _silu(x):
    return x * _sigmoid(x)


def _split2(x):
    hi = x.astype(BF16)
    lo = (x - hi.astype(F32)).astype(BF16)
    return hi, lo


def _split3(x):
    hi = x.astype(BF16)
    r = x - hi.astype(F32)
    mid = r.astype(BF16)
    lo = (r - mid.astype(F32)).astype(BF16)
    return hi, mid, lo


def _mm(a, b):
    return jnp.dot(a, b, preferred_element_type=F32)


def _mm_nt(a, b):
    return lax.dot_general(a, b, (((1,), (1,)), ((), ())), preferred_element_type=F32)


def _mm_tn(a, b):
    return lax.dot_general(a, b, (((0,), (0,)), ((), ())), preferred_element_type=F32)


def _mm_x2(a, b):
    ah, al = _split2(a)
    bh, bl = _split2(b)
    return _mm(ah, bh) + _mm(ah, bl) + _mm(al, bh)


def _mm_exact_lhs(a_bf, b):
    bh, bm, bl = _split3(b)
    return _mm(a_bf, bh) + _mm(a_bf, bm) + _mm(a_bf, bl)


def _mm_exact_rhs(a, b_bf):
    ah, am, al = _split3(a)
    return _mm(ah, b_bf) + _mm(am, b_bf) + _mm(al, b_bf)


def _layer_stats(x):
    mu = jnp.mean(x, axis=-1, keepdims=True)
    xc = x - mu
    var = jnp.mean(xc * xc, axis=-1, keepdims=True)
    return xc * lax.rsqrt(var + LN_EPS)


def _mod_kernel(c_ref, w_ref, b_ref, o_ref):
    s = _silu(c_ref[...])
    hi, lo = _split2(s)
    w = w_ref[0].astype(BF16)
    o_ref[0] = _mm(hi, w) + _mm(lo, w) + b_ref[0]


def _modulation(cc, w_mod, b_mod):
    depth, d, n6 = w_mod.shape
    tn = 1024
    return pl.pallas_call(
        _mod_kernel,
        grid=(depth, n6 // tn),
        in_specs=[pl.BlockSpec((8, d), lambda l, j: (0, 0)),
                  pl.BlockSpec((1, d, tn), lambda l, j: (l, 0, j)),
                  pl.BlockSpec((1, 1, tn), lambda l, j: (l, 0, j))],
        out_specs=pl.BlockSpec((1, 8, tn), lambda l, j: (l, 0, j)),
        out_shape=jax.ShapeDtypeStruct((depth, 8, n6), F32),
        compiler_params=_cparams(("arbitrary", "arbitrary")),
        name="modulation",
    )(cc, w_mod, b_mod.reshape(depth, 1, n6))


def _inproj_kernel(x_ref, sh_ref, sc_ref, w_ref, o_ref, u_scr):
    @pl.when(pl.program_id(1) == 0)
    def _():
        y = _layer_stats(x_ref[...]) * (1.0 + sc_ref[0]) + sh_ref[0]
        u_scr[...] = y.astype(BF16)

    o_ref[...] = _mm(u_scr[...], w_ref[0])


def _inproj(x, modrows, w, layer, *, seg_rows, row0):
    n, d = x.shape
    tm = min(512, n, seg_rows)
    tn = INPROJ_TN
    tiles_per_seg = seg_rows // tm

    def mrow(k):
        return lambda i, j: ((row0 + i // tiles_per_seg) * 6 + k, 0, 0)

    return pl.pallas_call(
        _inproj_kernel,
        grid=(n // tm, NP // tn),
        in_specs=[pl.BlockSpec((tm, d), lambda i, j: (i, 0)),
                  pl.BlockSpec((1, 1, d), mrow(0)),
                  pl.BlockSpec((1, 1, d), mrow(1)),
                  pl.BlockSpec((1, d, tn), lambda i, j: (layer, 0, j))],
        out_specs=pl.BlockSpec((tm, tn), lambda i, j: (i, j)),
        out_shape=jax.ShapeDtypeStruct((n, NP), F32),
        scratch_shapes=[pltpu.VMEM((tm, d), BF16)],
        compiler_params=_cparams(("arbitrary", "arbitrary")),
        name="inproj",
    )(x, modrows, modrows, w)


def _outproj_kernel(m0, m1, m2, m3, w_ref, x_ref, gate_ref, g_ref, b_ref, o_ref):
    gw = GROUP_WIDTH
    acc = _mm(m0[...], w_ref[0, 0 * gw:1 * gw, :])
    acc += _mm(m1[...], w_ref[0, 1 * gw:2 * gw, :])
    acc += _mm(m2[...], w_ref[0, 2 * gw:3 * gw, :])
    acc += _mm(m3[...], w_ref[0, 3 * gw:4 * gw, :])
    y = DEEPNORM_ALPHA * x_ref[...] + gate_ref[0] * acc
    o_ref[...] = _layer_stats(y) * g_ref[...] + b_ref[...]


def _outproj(mixes, w, layer, x, modrows, g, b, *, seg_rows, row0):
    n, d = x.shape
    tm = min(512, n, seg_rows)
    tiles_per_seg = seg_rows // tm
    mspec = pl.BlockSpec((tm, GROUP_WIDTH), lambda i: (i, 0))
    return pl.pallas_call(
        _outproj_kernel,
        grid=(n // tm,),
        in_specs=[mspec, mspec, mspec, mspec,
                  pl.BlockSpec((1, d, d), lambda i: (layer, 0, 0)),
                  pl.BlockSpec((tm, d), lambda i: (i, 0)),
                  pl.BlockSpec((1, 1, d), lambda i: ((row0 + i // tiles_per_seg) * 6 + 2, 0, 0)),
                  pl.BlockSpec((1, d), lambda i: (0, 0)),
                  pl.BlockSpec((1, d), lambda i: (0, 0))],
        out_specs=pl.BlockSpec((tm, d), lambda i: (i, 0)),
        out_shape=jax.ShapeDtypeStruct((n, d), F32),
        compiler_params=_cparams(("arbitrary",)),
        name="outproj",
    )(*mixes, w, x, modrows, g.reshape(1, d), b.reshape(1, d))


def _bwd_chunk_index(i, n_ctx_chunks, n_chunks):
    return jnp.where(i < n_ctx_chunks, n_ctx_chunks - 1 - i, n_chunks + n_ctx_chunks - 1 - i)


def _rms_gate_store(of_s, ob_s, g_ref, gate_fn, gate_refs, out_refs, seg_bounds):
    for (lo, hi), gate_ref, out_ref in zip(seg_bounds, gate_refs, out_refs):
        n = hi - lo
        tile = min(512, n)
        for r0 in range(0, n, tile):
            o = of_s[lo + r0: lo + r0 + tile, :] + ob_s[lo + r0: lo + r0 + tile, :]
            y = o * lax.rsqrt(jnp.mean(o * o, axis=-1, keepdims=True) + NORM_EPS) * g_ref[...]
            out_ref[r0:r0 + tile, :] = (y * gate_fn(gate_ref[r0:r0 + tile, :])).astype(out_ref.dtype)


def _select_col(x, lane, c):
    return jnp.sum(jnp.where(lane == c, x, 0.0), axis=1, keepdims=True)


def _rows8(row):
    return jnp.broadcast_to(row, (SUBLANES, LANES))


def _gdn_kernel(ql, kl, vl, zl, sl, qc, kc, vc, zc, sc, wq, wk, wv, alog, dtb, gn,
                ol, oc, q_s, k_s, v_s, tok_s, pad_s, u_s, w_s, qg_s, kd_s, p_s, ge_s, *, n_ctx, n_lat):
    h = pl.program_id(1)
    n_tot = n_ctx + n_lat
    segs = ((0, n_ctx), (n_ctx, n_tot))

    def conv_into(x_ref, w_ref, dst, lo, n, l2_scale):
        pad_s[0:8, :] = jnp.zeros((8, LANES), F32)
        pad_s[8:8 + n, :] = x_ref[...]
        pad_s[8 + n:16 + n, :] = jnp.zeros((8, LANES), F32)
        tile = min(512, n)
        for r0 in range(0, n, tile):
            acc = jnp.zeros((tile, LANES), F32)
            for i in range(GDN_CONV):
                s0 = 8 + r0 + i - GDN_CONV // 2
                acc = acc + pad_s[s0:s0 + tile, :] * w_ref[i:i + 1, :]
            y = _silu(acc)
            if l2_scale is not None:
                y = y * (lax.rsqrt(jnp.sum(y * y, axis=-1, keepdims=True) + NORM_EPS) * l2_scale)
            dst[lo + r0: lo + r0 + tile, :] = y

    for (lo, hi), (xq, xk, xv) in zip(segs, ((qc, kc, vc), (ql, kl, vl))):
        conv_into(xq, wq, q_s, lo, hi - lo, HEAD_DIM ** -0.5)
        conv_into(xk, wk, k_s, lo, hi - lo, 1.0)
        conv_into(xv, wv, v_s, lo, hi - lo, None)

    def tok_into(s_ref, lo, n):
        tile = min(512, n)
        lane = lax.broadcasted_iota(I32, (tile, LANES), 1)
        for r0 in range(0, n, tile):
            sm = s_ref[r0:r0 + tile, :]
            cols = []
            for d in range(2):
                beta = _sigmoid(_select_col(sm, lane, SM_BETA + d * GROUP_HEADS + h))
                dec = _select_col(sm, lane, SM_DEC + d * GROUP_HEADS + h)
                a_neg = -jnp.exp(_select_col(alog[...], lane[0:1], d * GROUP_HEADS + h))
                bias = _select_col(dtb[...], lane[0:1], d * GROUP_HEADS + h)
                la = a_neg * jax.nn.softplus(dec + bias)
                cols += [beta, la]
            t = jnp.where(lane == 0, cols[0], jnp.where(lane == 1, cols[1],
                          jnp.where(lane == 2, cols[2], jnp.where(lane == 3, cols[3], 0.0))))
            tok_s[lo + r0: lo + r0 + tile, :] = t

    tok_into(sc, 0, n_ctx)
    tok_into(sl, n_ctx, n_lat)

    c = GDN_CHUNK
    nb = ROWS // c
    r = lax.broadcasted_iota(I32, (ROWS, ROWS), 0)
    cc = lax.broadcasted_iota(I32, (ROWS, ROWS), 1)
    same = (r // c) == (cc // c)
    same_bf = same.astype(BF16)
    eye = r == cc
    eye_f = eye.astype(F32)

    def phase1(si, _):
        off = pl.multiple_of(si * ROWS, ROWS)
        q = q_s[pl.ds(off, ROWS), :]
        k = k_s[pl.ds(off, ROWS), :]
        v = v_s[pl.ds(off, ROWS), :]
        tk = tok_s[pl.ds(off, ROWS), :]
        qb = q.astype(BF16)
        kb = k.astype(BF16)
        kk = _mm_nt(kb, kb)
        qk = _mm_nt(qb, kb)
        dirs = (0, 1)
        beta = [tk[:, 2 * d:2 * d + 1] for d in dirs]
        incl = [same & ((cc >= r) if d else (cc <= r)) for d in dirs]
        strict = [same & ((cc > r) if d else (cc < r)) for d in dirs]
        gi_full = [_mm_exact_lhs(incl[d].astype(BF16), jnp.broadcast_to(tk[:, 2 * d + 1:2 * d + 2], (ROWS, LANES)))
                   for d in dirs]
        gi = [jnp.concatenate([gi_full[d], gi_full[d]], axis=1) for d in dirs]
        gj = [_mm_exact_lhs(same_bf, jnp.where(eye, gi[d], 0.0)) for d in dirs]
        diff = [gi[d] - gj[d] for d in dirs]
        a = [beta[d] * kk * jnp.exp(jnp.where(strict[d], diff[d], -jnp.inf)) for d in dirs]
        t = [eye_f - a[d] for d in dirs]
        pb = [a[d].astype(BF16) for d in dirs]
        for _ in range(5):
            pb = [_mm(pb[d], pb[d]).astype(BF16) for d in dirs]
            t = [t[d] + _mm(t[d].astype(BF16), pb[d]) for d in dirs]
        eg = [jnp.exp(gi_full[d]) for d in dirs]
        sol = [_mm_x2(t[d], jnp.concatenate([beta[d] * v, (beta[d] * eg[d]) * k], axis=1)) for d in dirs]
        for d in dirs:
            p = qk * jnp.exp(jnp.where(incl[d], diff[d], -jnp.inf))
            for kb_ in range(nb):
                p_s[d, pl.ds(pl.multiple_of(off + kb_ * c, c), c), :] = (
                    p[kb_ * c:(kb_ + 1) * c, kb_ * c:(kb_ + 1) * c].astype(BF16))
            u_s[d, pl.ds(off, ROWS), :] = sol[d][:, :LANES]
            w_s[d, pl.ds(off, ROWS), :] = sol[d][:, LANES:].astype(BF16)
            gr = gi_full[d].reshape(nb, c, LANES)
            gl = gr[:, 0:1, :] if d else gr[:, c - 1:c, :]
            g_last = jnp.broadcast_to(gl, (nb, c, LANES)).reshape(ROWS, LANES)
            qg_s[d, pl.ds(off, ROWS), :] = (q * eg[d]).astype(BF16)
            kd_s[d, pl.ds(off, ROWS), :] = (k * jnp.exp(g_last - gi_full[d])).astype(BF16)
            ge = jnp.broadcast_to(jnp.exp(gl), (nb, SUBLANES, LANES)).reshape(nb * SUBLANES, LANES)
            ge_s[d, pl.ds(pl.multiple_of(si * nb * SUBLANES, nb * SUBLANES), nb * SUBLANES), :] = ge
        return 0

    lax.fori_loop(0, n_tot // ROWS, phase1, 0)

    of_s, ob_s = q_s, k_s
    n_cc = n_ctx // c
    n_ch = n_tot // c

    def step(i, carry):
        dirs = (0, 1)
        chs = (i, _bwd_chunk_index(i, n_cc, n_ch))
        offs = [pl.multiple_of(chs[d] * c, c) for d in dirs]
        ge = [ge_s[d, pl.ds(pl.multiple_of(chs[d] * SUBLANES, SUBLANES), SUBLANES), :][0:1, :] for d in dirs]
        wq = [jnp.concatenate([w_s[d, pl.ds(offs[d], c), :], qg_s[d, pl.ds(offs[d], c), :]], axis=0) for d in dirs]
        r1 = [_mm(wq[d], carry[d].astype(BF16)) for d in dirs]
        vb = [(u_s[d, pl.ds(offs[d], c), :] - r1[d][:c]).astype(BF16) for d in dirs]
        upd = [_mm_tn(kd_s[d, pl.ds(offs[d], c), :], vb[d]) for d in dirs]
        o = [r1[d][c:] + _mm(p_s[d, pl.ds(offs[d], c), :], vb[d]) for d in dirs]
        of_s[pl.ds(offs[0], c), :] = o[0]
        ob_s[pl.ds(offs[1], c), :] = o[1]
        return tuple(ge[d] * carry[d] + upd[d] for d in dirs)

    z0 = jnp.zeros((HEAD_DIM, HEAD_DIM), F32)
    lax.fori_loop(0, n_ch, step, (z0, z0))

    _rms_gate_store(of_s, ob_s, gn, _silu, (zc, zl), (oc, ol), segs)


def _gdn(p_lat, p_ctx, conv_w, a_log, dt_bias, gnorm, *, batch):
    n_lat = p_lat.shape[0] // batch
    n_ctx = p_ctx.shape[0] // batch
    n_tot = n_lat + n_ctx
    assert n_lat % ROWS == 0 and n_ctx % ROWS == 0
    n_ch = n_tot // GDN_CHUNK

    def blk(n, base):
        return pl.BlockSpec((n, LANES), lambda b, h: (b, base + h))

    def small(n):
        return pl.BlockSpec((n, LANES), lambda b, h: (b, BLK["small"]))

    def cw(base):
        return pl.BlockSpec((8, LANES), lambda b, h: (0, base + h))

    row = pl.BlockSpec((1, LANES), lambda b, h: (0, 0))
    conv_p = jnp.zeros((8, 3 * GROUP_WIDTH), F32).at[:GDN_CONV].set(conv_w)
    alog_row = jnp.zeros((1, LANES), F32).at[0, :2 * GROUP_HEADS].set(a_log.reshape(-1))
    dtb_row = jnp.zeros((1, LANES), F32).at[0, :2 * GROUP_HEADS].set(dt_bias.reshape(-1))
    lat_in = [blk(n_lat, BLK[k]) for k in ("a_q", "a_k", "a_v", "a_z")] + [small(n_lat)]
    ctx_in = [blk(n_ctx, BLK[k]) for k in ("a_q", "a_k", "a_v", "a_z")] + [small(n_ctx)]
    kern = functools.partial(_gdn_kernel, n_ctx=n_ctx, n_lat=n_lat)
    return pl.pallas_call(
        kern,
        grid=(batch, GROUP_HEADS),
        in_specs=lat_in + ctx_in + [cw(0), cw(4), cw(8), row, row, row],
        out_specs=[pl.BlockSpec((n_lat, LANES), lambda b, h: (b, h)),
                   pl.BlockSpec((n_ctx, LANES), lambda b, h: (b, h))],
        out_shape=[jax.ShapeDtypeStruct((batch * n_lat, GROUP_WIDTH), BF16),
                   jax.ShapeDtypeStruct((batch * n_ctx, GROUP_WIDTH), BF16)],
        scratch_shapes=[pltpu.VMEM((n_tot, LANES), F32)] * 4
        + [pltpu.VMEM((n_lat + 16, LANES), F32),
           pltpu.VMEM((2, n_tot, LANES), F32),
           pltpu.VMEM((2, n_tot, LANES), BF16),
           pltpu.VMEM((2, n_tot, LANES), BF16),
           pltpu.VMEM((2, n_tot, LANES), BF16),
           pltpu.VMEM((2, n_tot, GDN_CHUNK), BF16),
           pltpu.VMEM((2, n_ch * SUBLANES, LANES), F32)],
        compiler_params=_cparams(("arbitrary", "arbitrary")),
        name="gdn",
    )(*([p_lat] * 5), *([p_ctx] * 5), conv_p, conv_p, conv_p, alog_row, dtb_row, gnorm.reshape(1, LANES))


DROWS = 128
N_LEVELS = 7


def _cumsum_matrices():
    i = np.arange(DROWS)[:, None]
    u = np.arange(DROWS)[None, :]
    return np.stack([(u <= i), (u >= i)]).astype(np.float32)


_LOW_MATS = _cumsum_matrices()


def _level_index(rev):
    r = lax.broadcasted_iota(I32, (DROWS, DROWS), 0)
    cc = lax.broadcasted_iota(I32, (DROWS, DROWS), 1)
    lev = 31 - lax.clz(r ^ cc)
    earlier = (cc > r) if rev else (cc < r)
    return jnp.where(earlier, lev, jnp.where(r == cc, -1, -2))


def _rows_at(b, rowi, m, shifts, cache):
    c = b.shape[0]
    out = None
    for mval, sh in enumerate(shifts):
        if sh not in cache:
            cache[sh] = b if sh == 0 else pltpu.roll(b, sh % c, 0)
        out = cache[sh] if out is None else jnp.where(m == mval, cache[sh], out)
    src = rowi - sum(jnp.where(m == mval, sh, 0) for mval, sh in enumerate(shifts))
    return jnp.where((src >= 0) & (src < c), out, 0.0)


def _block_sums(b, rev, rowi):
    c = DROWS
    win, wout, rolled = {}, {}, {}
    for lv in (1, 2):
        s = 1 << lv
        m = rowi & (s - 1)
        if rev:
            win[lv] = b - _rows_at(b, rowi, m, [-(s - t) for t in range(s)], rolled)
            wout[lv] = _rows_at(b, rowi, m, list(range(s)), rolled) - b
        else:
            win[lv] = b - _rows_at(b, rowi, m, [t + 1 for t in range(s)], rolled)
            wout[lv] = _rows_at(b, rowi, m, [-(s - 1 - t) for t in range(s)], rolled) - b
    for lv in range(3, N_LEVELS + 1):
        s = 1 << lv
        nblk = c // s
        br = b.reshape(nblk, s, LANES)
        edge = br[:, 0:1, :] if rev else br[:, s - 1:s, :]
        zero = jnp.zeros((1, 1, LANES), F32)
        if nblk == 1:
            before = zero
        elif rev:
            before = jnp.concatenate([edge[1:], zero], axis=0)
        else:
            before = jnp.concatenate([zero, edge[:-1]], axis=0)
        win[lv] = (br - before).reshape(c, LANES)
        wout[lv] = (edge - br).reshape(c, LANES)
    return win, wout


def _diag_phase1(probs, low_ref, levs):
    c = DROWS
    idx = range(len(probs))
    rowi = lax.broadcasted_iota(I32, (c, LANES), 0)
    bs = [_mm_exact_lhs(low_ref[1 if p[4] else 0], p[3]) for p in probs]
    sums = [_block_sums(bs[i], probs[i][4], rowi) for i in idx]
    qb = [p[0].astype(BF16) for p in probs]
    kb = [p[1].astype(BF16) for p in probs]
    lev = [levs[1 if p[4] else 0] for p in probs]
    pm = [jnp.where(lev[i] == -1, _mm_nt(qb[i], kb[i]), 0.0) for i in idx]
    for lv in range(N_LEVELS):
        qs = [(probs[i][0] * jnp.exp(probs[i][3] if lv == 0 else sums[i][0][lv])).astype(BF16) for i in idx]
        ks = [kb[i] if lv == 0 else (probs[i][1] * jnp.exp(sums[i][1][lv])).astype(BF16) for i in idx]
        dots = [_mm_nt(qs[i], ks[i]) for i in idx]
        pm = [jnp.where(lev[i] == lv, dots[i], pm[i]) for i in idx]
    o_intra = [_mm(pm[i].astype(BF16), probs[i][2]) for i in idx]
    qg = [(probs[i][0] * jnp.exp(sums[i][0][N_LEVELS])).astype(BF16) for i in idx]
    kd = [(probs[i][1] * jnp.exp(sums[i][1][N_LEVELS])).astype(BF16) for i in idx]
    a_end = [jnp.exp(bs[i][0:1, :] if probs[i][4] else bs[i][c - 1:c, :]) for i in idx]
    return [(o_intra[i], qg[i], kd[i], a_end[i]) for i in idx]


DIAG_CHUNKS_PER_STEP = 2


def _diag_run_phase1(load, lo, n_rows, low_ref, levs, v_bs, o_s, qg_s, kd_s, ae_s):
    def step(pi, _):
        probs, offs = [], []
        for u in range(DIAG_CHUNKS_PER_STEP):
            r0 = pl.multiple_of((pi * DIAG_CHUNKS_PER_STEP + u) * DROWS, DROWS)
            q, kf, kb, v, laf, lab = load(r0)
            off = pl.multiple_of(lo + r0, DROWS)
            v_bf = v.astype(BF16)
            v_bs[pl.ds(off, DROWS), :] = v_bf
            probs += [(q, kf, v_bf, laf, False), (q, kb, v_bf, lab, True)]
            offs += [(off, 0), (off, 1)]
        for (off, d), (o_intra, qg, kd, a_end) in zip(offs, _diag_phase1(probs, low_ref, levs)):
            o_s[d][pl.ds(off, DROWS), :] = o_intra
            qg_s[d, pl.ds(off, DROWS), :] = qg
            kd_s[d, pl.ds(off, DROWS), :] = kd
            ci = off // DROWS
            ae_s[d, pl.ds(pl.multiple_of(ci * SUBLANES, SUBLANES), SUBLANES), :] = _rows8(a_end)
        return 0

    lax.fori_loop(0, n_rows // (DIAG_CHUNKS_PER_STEP * DROWS), step, 0)


def _diag_phase2(v_bs, o_s, qg_s, kd_s, ae_s, n_ctx, n_tot):
    n_cc = n_ctx // DROWS
    n_ch = n_tot // DROWS

    def step(i, carry):
        dirs = (0, 1)
        chs = (i, _bwd_chunk_index(i, n_cc, n_ch))
        offs = [pl.multiple_of(chs[d] * DROWS, DROWS) for d in dirs]
        a_end = [ae_s[d, pl.ds(pl.multiple_of(chs[d] * SUBLANES, SUBLANES), SUBLANES), :][0:1, :] for d in dirs]
        stb = [carry[d].astype(BF16) for d in dirs]
        inter = [_mm_nt(qg_s[d, pl.ds(offs[d], DROWS), :], stb[d]) for d in dirs]
        upd = [_mm_tn(v_bs[pl.ds(offs[d], DROWS), :], kd_s[d, pl.ds(offs[d], DROWS), :]) for d in dirs]
        for d in dirs:
            o_s[d][pl.ds(offs[d], DROWS), :] += inter[d]
        return tuple(carry[d] * a_end[d] + upd[d] for d in dirs)

    z0 = jnp.zeros((HEAD_DIM, HEAD_DIM), F32)
    lax.fori_loop(0, n_ch, step, (z0, z0))


def _diag_scratch(n_tot):
    n_ch = n_tot // DROWS
    return [pltpu.VMEM((n_tot, LANES), BF16),
            pltpu.VMEM((n_tot, LANES), F32),
            pltpu.VMEM((n_tot, LANES), F32),
            pltpu.VMEM((2, n_tot, LANES), BF16),
            pltpu.VMEM((2, n_tot, LANES), BF16),
            pltpu.VMEM((2, n_ch * SUBLANES, LANES), F32)]


def _gla_kernel(ql, kl, vl, rl, sl, qc, kc, vc, rc, sc, cos_ref, sin_ref, perm_ref, wg_ref, bg_ref, gn,
                low_ref, ol, oc, v_bs, of_s, ob_s, qg_s, kd_s, ae_s, *, n_ctx, n_lat):
    n_tot = n_ctx + n_lat
    segs = ((0, n_ctx), (n_ctx, n_tot))
    levs = (_level_index(False), _level_index(True))

    for (lo, hi), (xq, xk, xv, xs), rope in zip(segs, ((qc, kc, vc, sc), (ql, kl, vl, sl)), (False, True)):
        def load(r0, xq=xq, xk=xk, xv=xv, xs=xs, rope=rope):
            q = xq[pl.ds(r0, DROWS), :]
            k = xk[pl.ds(r0, DROWS), :]
            if rope:
                cs = cos_ref[pl.ds(r0, DROWS), :]
                sn = sin_ref[pl.ds(r0, DROWS), :]
                q = q * cs + _mm_exact_rhs(q, perm_ref[...]) * sn
                k = k * cs + _mm_exact_rhs(k, perm_ref[...]) * sn
            sm = xs[pl.ds(r0, DROWS), :]
            las = [jax.nn.log_sigmoid(_mm_x2(sm, wg_ref[d, 0]) + bg_ref[d, 0]) * (1.0 / GLA_TAU) for d in range(2)]
            return q * (GLA_DK ** -0.5), k, k, xv[pl.ds(r0, DROWS), :], las[0], las[1]

        _diag_run_phase1(load, lo, hi - lo, low_ref, levs, v_bs, (of_s, ob_s), qg_s, kd_s, ae_s)

    _diag_phase2(v_bs, (of_s, ob_s), qg_s, kd_s, ae_s, n_ctx, n_tot)
    _rms_gate_store(of_s, ob_s, gn, _silu, (rc, rl), (oc, ol), segs)


def _rope_tables(n_lat):
    t = np.arange(n_lat)
    nf = GLA_DK // 4
    inv = (ROPE_BASE ** (-jnp.arange(nf, dtype=F32) / nf))
    lane = np.arange(LANES)
    f = lane % nf
    use_col = (lane % GLA_DK) >= GLA_DK // 2
    first = (lane % (2 * nf)) < nf
    real = lane < GLA_DK
    pos = jnp.where(use_col[None, :], (t % GRID_W)[:, None], (t // GRID_W)[:, None]).astype(F32)
    ang = pos * inv[f][None, :]
    cos = jnp.where(real[None, :], jnp.cos(ang), 1.0)
    sin = jnp.where(real[None, :], jnp.where(first[None, :], -jnp.sin(ang), jnp.sin(ang)), 0.0)
    partner = np.where(first, lane + nf, lane - nf)
    perm = np.zeros((LANES, LANES), np.float32)
    perm[partner[real], lane[real]] = 1.0
    return cos.astype(F32), sin.astype(F32), jnp.asarray(perm, BF16)


def _gla(p_lat, p_ctx, w_gate, b_gate, gnorm, low, *, batch):
    n_lat = p_lat.shape[0] // batch
    n_ctx = p_ctx.shape[0] // batch
    n_tot = n_lat + n_ctx
    assert n_lat % DROWS == 0 and n_ctx % DROWS == 0
    cos, sin, perm = _rope_tables(n_lat)
    wg = jnp.zeros((2, GROUP_HEADS, LANES, LANES), F32)
    wsrc = w_gate.reshape(2, GLA_RANK, GROUP_HEADS, GLA_DK).transpose(0, 2, 1, 3)
    for d in range(2):
        wg = wg.at[d, :, SM_CODE + d * GLA_RANK: SM_CODE + (d + 1) * GLA_RANK, :GLA_DK].set(wsrc[d])
    bg = jnp.zeros((2, GROUP_HEADS, 1, LANES), F32).at[:, :, 0, :GLA_DK].set(
        b_gate.reshape(2, GROUP_HEADS, GLA_DK))

    def blk(n, base):
        return pl.BlockSpec((n, LANES), lambda b, h: (b, base + h))

    def small(n):
        return pl.BlockSpec((n, LANES), lambda b, h: (b, BLK["small"]))

    def const2(shape):
        return pl.BlockSpec(shape, lambda b, h: (0, 0))

    names = ("b_q", "b_k", "b_v", "b_r")
    kern = functools.partial(_gla_kernel, n_ctx=n_ctx, n_lat=n_lat)
    return pl.pallas_call(
        kern,
        grid=(batch, GROUP_HEADS),
        in_specs=[blk(n_lat, BLK[k]) for k in names] + [small(n_lat)]
        + [blk(n_ctx, BLK[k]) for k in names] + [small(n_ctx)]
        + [const2((n_lat, LANES)), const2((n_lat, LANES)), const2((LANES, LANES)),
           pl.BlockSpec((2, 1, LANES, LANES), lambda b, h: (0, h, 0, 0)),
           pl.BlockSpec((2, 1, 1, LANES), lambda b, h: (0, h, 0, 0)),
           const2((1, LANES)),
           pl.BlockSpec(low.shape, lambda b, h: (0, 0, 0))],
        out_specs=[pl.BlockSpec((n_lat, LANES), lambda b, h: (b, h)),
                   pl.BlockSpec((n_ctx, LANES), lambda b, h: (b, h))],
        out_shape=[jax.ShapeDtypeStruct((batch * n_lat, GROUP_WIDTH), BF16),
                   jax.ShapeDtypeStruct((batch * n_ctx, GROUP_WIDTH), BF16)],
        scratch_shapes=_diag_scratch(n_tot),
        compiler_params=_cparams(("arbitrary", "arbitrary")),
        name="gla",
    )(*([p_lat] * 5), *([p_ctx] * 5), cos, sin, perm, wg, bg, gnorm.reshape(1, LANES), low)


def _hgrn_kernel(ql, il, gl, f0l, f1l, qc, ic, gc, f0c, f1c, gam_ref, gn, low_ref,
                 ol, oc, v_bs, of_s, ob_s, qg_s, kd_s, ae_s, *, n_ctx, n_lat, layer):
    n_tot = n_ctx + n_lat
    segs = ((0, n_ctx), (n_ctx, n_tot))
    levs = (_level_index(False), _level_index(True))
    lbs = []
    for d in range(2):
        gam = gam_ref[d]
        ex = jnp.exp(gam - jnp.max(gam, axis=0, keepdims=True))
        pr = ex / jnp.sum(ex, axis=0, keepdims=True)
        lb = jnp.zeros((1, LANES), F32)
        for m in range(1, layer + 1):
            lb = lb + pr[m:m + 1, :]
        lbs.append(lb)

    for (lo, hi), (xq, xi, xf0, xf1) in zip(segs, ((qc, ic, f0c, f1c), (ql, il, f0l, f1l))):
        def load(r0, xq=xq, xi=xi, xf0=xf0, xf1=xf1):
            fs = [lbs[d] + (1.0 - lbs[d]) * _sigmoid(xf[pl.ds(r0, DROWS), :]) for d, xf in enumerate((xf0, xf1))]
            return (xq[pl.ds(r0, DROWS), :], 1.0 - fs[0], 1.0 - fs[1], xi[pl.ds(r0, DROWS), :],
                    jnp.log(fs[0]), jnp.log(fs[1]))

        _diag_run_phase1(load, lo, hi - lo, low_ref, levs, v_bs, (of_s, ob_s), qg_s, kd_s, ae_s)

    _diag_phase2(v_bs, (of_s, ob_s), qg_s, kd_s, ae_s, n_ctx, n_tot)
    _rms_gate_store(of_s, ob_s, gn, _sigmoid, (gc, gl), (oc, ol), segs)


def _hgrn(p_lat, p_ctx, gamma, gnorm, low, *, batch, layer):
    n_lat = p_lat.shape[0] // batch
    n_ctx = p_ctx.shape[0] // batch
    n_tot = n_lat + n_ctx
    assert n_lat % DROWS == 0 and n_ctx % DROWS == 0

    def blk(n, base):
        return pl.BlockSpec((n, LANES), lambda b, h: (b, base + h))

    names = ("c_q", "c_i", "c_g", "c_f0", "c_f1")
    kern = functools.partial(_hgrn_kernel, n_ctx=n_ctx, n_lat=n_lat, layer=layer)
    return pl.pallas_call(
        kern,
        grid=(batch, GROUP_HEADS),
        in_specs=[blk(n_lat, BLK[k]) for k in names] + [blk(n_ctx, BLK[k]) for k in names]
        + [pl.BlockSpec((2, gamma.shape[1], LANES), lambda b, h: (0, 0, h)),
           pl.BlockSpec((1, LANES), lambda b, h: (0, 0)),
           pl.BlockSpec(low.shape, lambda b, h: (0, 0, 0))],
        out_specs=[pl.BlockSpec((n_lat, LANES), lambda b, h: (b, h)),
                   pl.BlockSpec((n_ctx, LANES), lambda b, h: (b, h))],
        out_shape=[jax.ShapeDtypeStruct((batch * n_lat, GROUP_WIDTH), BF16),
                   jax.ShapeDtypeStruct((batch * n_ctx, GROUP_WIDTH), BF16)],
        scratch_shapes=_diag_scratch(n_tot),
        compiler_params=_cparams(("arbitrary", "arbitrary")),
        name="hgrn",
    )(*([p_lat] * 5), *([p_ctx] * 5), gamma, gnorm.reshape(1, LANES), low)


NA_GROUP = 4
NA_WIN = NA_ROWS + NA_GROUP


def _na_kernel(ws_ref, pat_ref, ql, kl, vl, qc, kc, vc, bias_ref, ol, oc, *, n_groups, win):
    scale = HEAD_DIM ** -0.5
    kcb = kc[...].astype(BF16)
    vcb = vc[...].astype(BF16)
    w = GRID_W
    gq = NA_GROUP * w

    per_step = 2 if n_groups % 2 == 0 else 1

    def groups(it, _):
        gs = [it * per_step + u for u in range(per_step)]
        us = range(per_step)
        qoff = [pl.multiple_of(g * gq, gq) for g in gs]
        koff = [pl.multiple_of(ws_ref[g] * w, w) for g in gs]
        q = [ql[pl.ds(qoff[u], gq), :].astype(BF16) for u in us]
        kb = [kl[pl.ds(koff[u], win * w), :].astype(BF16) for u in us]
        vb = [vl[pl.ds(koff[u], win * w), :].astype(BF16) for u in us]
        s_loc = [_mm_nt(q[u], kb[u]) * scale + bias_ref[0, pat_ref[gs[u]]] for u in us]
        s_ctx = [_mm_nt(q[u], kcb) * scale for u in us]
        m = [jnp.maximum(jnp.max(s_loc[u], axis=-1, keepdims=True), jnp.max(s_ctx[u], axis=-1, keepdims=True))
             for u in us]
        p_loc = [jnp.exp(s_loc[u] - m[u]) for u in us]
        p_ctx = [jnp.exp(s_ctx[u] - m[u]) for u in us]
        den = [jnp.sum(p_loc[u], axis=-1, keepdims=True) + jnp.sum(p_ctx[u], axis=-1, keepdims=True) for u in us]
        o = [_mm(p_loc[u].astype(BF16), vb[u]) + _mm(p_ctx[u].astype(BF16), vcb) for u in us]
        for u in us:
            ol[pl.ds(qoff[u], gq), :] = (o[u] / den[u]).astype(ol.dtype)
        return 0

    lax.fori_loop(0, n_groups // per_step, groups, 0)

    s = _mm_nt(qc[...].astype(BF16), kcb) * scale
    p = jnp.exp(s - jnp.max(s, axis=-1, keepdims=True))
    o = _mm(p.astype(BF16), vcb) / jnp.sum(p, axis=-1, keepdims=True)
    oc[...] = o.astype(oc.dtype)


def _na_plan(n_rows):
    kr = min(NA_ROWS, n_rows)
    win = min(NA_WIN, n_rows)
    n_groups = n_rows // NA_GROUP
    ws = np.clip(np.arange(n_groups) * NA_GROUP - kr // 2, 0, n_rows - win)
    seen, pats, pat_of = {}, [], []
    for g in range(n_groups):
        rows = g * NA_GROUP + np.arange(NA_GROUP)
        rs = np.clip(rows - kr // 2, 0, n_rows - kr)
        krow = ws[g] + np.arange(win)
        ok = (krow[None, :] >= rs[:, None]) & (krow[None, :] < rs[:, None] + kr)
        dr = np.where(ok, krow[None, :] - rows[:, None] + NA_ROWS - 1, 0)
        key = (ok.tobytes(), dr.tobytes())
        if key not in seen:
            seen[key] = len(pats)
            pats.append((ok, dr))
        pat_of.append(seen[key])
    return ws.astype(np.int32), np.asarray(pat_of, np.int32), pats, win


def _na_bias(rpb, pats, win):
    n_h = rpb.shape[0]
    n_p = len(pats)
    ok = np.stack([p[0] for p in pats])
    dr = np.stack([p[1] for p in pats]).reshape(-1)
    row_sel = np.zeros((dr.size, 2 * NA_ROWS - 1), np.float32)
    row_sel[np.arange(dr.size), dr] = 1.0
    cq = np.arange(GRID_W)
    dc = np.clip(cq[None, :] - cq[:, None], -(NA_COLS - 1), NA_COLS - 1) + NA_COLS - 1
    col_start = np.clip(cq - NA_COLS // 2, 0, GRID_W - NA_COLS)
    col_in = (cq[None, :] >= col_start[:, None]) & (cq[None, :] < col_start[:, None] + NA_COLS)
    col_sel = np.zeros((2 * NA_COLS - 1, GRID_W * GRID_W), np.float32)
    col_sel[dc.reshape(-1), np.arange(GRID_W * GRID_W)] = 1.0
    hp = lax.Precision.HIGHEST
    b = jnp.einsum("xr,hrd->hxd", row_sel, rpb.astype(F32), precision=hp)
    b = jnp.einsum("hxd,dq->hxq", b, col_sel, precision=hp)
    b = b.reshape(n_h, n_p, NA_GROUP, win, GRID_W, GRID_W).transpose(0, 1, 2, 4, 3, 5)
    valid = ok[:, :, None, :, None] & col_in[None, None, :, None, :]
    b = jnp.where(valid[None], b, NEG_BIG)
    return b.reshape(n_h, n_p, NA_GROUP * GRID_W, win * GRID_W)


def _na(p_lat, p_ctx, rpb, *, batch):
    n_lat = p_lat.shape[0] // batch
    n_ctx = p_ctx.shape[0] // batch
    n_rows = n_lat // GRID_W
    assert n_rows % NA_GROUP == 0
    ws, pat_of, pats, win = _na_plan(n_rows)
    bias = _na_bias(rpb, pats, win)

    def blk(n, base):
        return pl.BlockSpec((n, LANES), lambda b, h, *_: (b, base + h))

    names = ("d_q", "d_k", "d_v")
    return pl.pallas_call(
        functools.partial(_na_kernel, n_groups=n_rows // NA_GROUP, win=win),
        grid_spec=pltpu.PrefetchScalarGridSpec(
            num_scalar_prefetch=2,
            grid=(batch, GROUP_HEADS),
            in_specs=[blk(n_lat, BLK[k]) for k in names] + [blk(n_ctx, BLK[k]) for k in names]
            + [pl.BlockSpec((1,) + bias.shape[1:], lambda b, h, *_: (h, 0, 0, 0))],
            out_specs=[pl.BlockSpec((n_lat, LANES), lambda b, h, *_: (b, h)),
                       pl.BlockSpec((n_ctx, LANES), lambda b, h, *_: (b, h))]),
        out_shape=[jax.ShapeDtypeStruct((batch * n_lat, GROUP_WIDTH), BF16),
                   jax.ShapeDtypeStruct((batch * n_ctx, GROUP_WIDTH), BF16)],
        compiler_params=_cparams(("arbitrary", "arbitrary")),
        name="na",
    )(jnp.asarray(ws), jnp.asarray(pat_of), *([p_lat] * 3), *([p_ctx] * 3), bias)


def _router_kernel(x_ref, sh_ref, sc_ref, wr_ref, u_ref, code_ref, gate_ref, cum_ref, lg_scr,
                   *, cap, n_tiles, tile):
    j = pl.program_id(1)
    u = _layer_stats(x_ref[...]) * (1.0 + sc_ref[0]) + sh_ref[0]
    u_ref[...] = u.astype(BF16)
    lg_scr[j] = lax.dot_general(wr_ref[...], u, (((1,), (1,)), ((), ())),
                                precision=lax.Precision.HIGHEST, preferred_element_type=F32)

    @pl.when(j == n_tiles - 1)
    def _():
        lg = lg_scr[...]
        ex = jnp.exp(lg - jnp.max(lg, axis=1, keepdims=True))
        aff = ex / jnp.sum(ex, axis=1, keepdims=True)
        bits = lax.bitcast_convert_type(aff, I32)

        def count(mask):
            per = jnp.sum(mask.astype(F32), axis=0)
            return jnp.sum(per, axis=1, keepdims=True)

        def bis(it, thr):
            cand = thr | jnp.left_shift(jnp.int32(1), 30 - it)
            ok = count(bits >= cand[None]) >= float(cap)
            return jnp.where(ok, cand, thr)

        thr = lax.fori_loop(0, 31, bis, jnp.zeros((N_EXPERTS, 1), I32))
        gt = bits > thr[None]
        eq = bits == thr[None]
        need = float(cap) - count(gt)
        r = lax.broadcasted_iota(I32, (tile, tile), 0)
        c = lax.broadcasted_iota(I32, (tile, tile), 1)
        upper = (r < c).astype(BF16)
        carry_eq = jnp.zeros((N_EXPERTS, 1), F32)
        carry_sel = jnp.zeros((N_EXPERTS, 1), F32)
        for t in range(n_tiles):
            eq_t = eq[t].astype(BF16)
            pre_eq = _mm(eq_t, upper) + carry_eq
            sel = gt[t] | (eq[t] & (pre_eq < need))
            sel_b = sel.astype(BF16)
            pos = _mm(sel_b, upper) + carry_sel
            code_ref[0, t] = jnp.where(sel, pos.astype(I32), -1)
            gate_ref[0, t] = aff[t]
            carry_eq = carry_eq + jnp.sum(eq_t.astype(F32), axis=1, keepdims=True)
            carry_sel = carry_sel + jnp.sum(sel_b.astype(F32), axis=1, keepdims=True)
            cum_ref[0, t] = jnp.broadcast_to(carry_sel, (N_EXPERTS, LANES)).astype(I32)


def _router(x, modrows, w_router, *, seg_rows, row0, row_stride):
    n, d = x.shape
    nseg = n // seg_rows
    tile = min(ROUTE_TILE, seg_rows)
    n_tiles = seg_rows // tile
    cap = EC_CAPACITY * seg_rows // N_EXPERTS
    kern = functools.partial(_router_kernel, cap=cap, n_tiles=n_tiles, tile=tile)
    return pl.pallas_call(
        kern,
        grid=(nseg, n_tiles),
        in_specs=[pl.BlockSpec((tile, d), lambda s, j: (s * n_tiles + j, 0)),
                  pl.BlockSpec((1, 1, d), lambda s, j: ((row0 + s * row_stride) * 6 + 3, 0, 0)),
                  pl.BlockSpec((1, 1, d), lambda s, j: ((row0 + s * row_stride) * 6 + 4, 0, 0)),
                  pl.BlockSpec((N_EXPERTS, d), lambda s, j: (0, 0))],
        out_specs=[pl.BlockSpec((tile, d), lambda s, j: (s * n_tiles + j, 0)),
                   pl.BlockSpec((1, n_tiles, N_EXPERTS, tile), lambda s, j: (s, 0, 0, 0)),
                   pl.BlockSpec((1, n_tiles, N_EXPERTS, tile), lambda s, j: (s, 0, 0, 0)),
                   pl.BlockSpec((1, n_tiles, N_EXPERTS, LANES), lambda s, j: (s, 0, 0, 0))],
        out_shape=[jax.ShapeDtypeStruct((n, d), BF16),
                   jax.ShapeDtypeStruct((nseg, n_tiles, N_EXPERTS, tile), I32),
                   jax.ShapeDtypeStruct((nseg, n_tiles, N_EXPERTS, tile), F32),
                   jax.ShapeDtypeStruct((nseg, n_tiles, N_EXPERTS, LANES), I32)],
        scratch_shapes=[pltpu.VMEM((n_tiles, N_EXPERTS, tile), F32)],
        compiler_params=_cparams(("arbitrary", "arbitrary")),
        name="router",
    )(x, modrows, modrows, w_router.T)


SLOT_TILE = 128


def _slot_tiles_touched(cum_ref, base, t, ts, n_slot_tiles):
    lo = cum_ref[base + t]
    hi = cum_ref[base + t + 1]
    return [(lo < (i + 1) * ts) & (hi > i * ts) for i in range(n_slot_tiles)]


def _gather_kernel(cum_ref, u_ref, code_ref, o_ref, acc, *, cap, n_tiles, tile, ts):
    base = (pl.program_id(0) * N_EXPERTS + pl.program_id(1)) * (n_tiles + 1)
    acc[...] = jnp.zeros_like(acc)
    slot = lax.broadcasted_iota(I32, (ts, tile), 0)
    for t in range(n_tiles):
        for i, touched in enumerate(_slot_tiles_touched(cum_ref, base, t, ts, cap // ts)):
            @pl.when(touched)
            def _(t=t, i=i):
                oh = (code_ref[0, 0, t:t + 1, :] == slot + i * ts).astype(BF16)
                acc[i * ts:(i + 1) * ts, :] += _mm(oh, u_ref[t * tile:(t + 1) * tile, :])
    o_ref[0, 0] = acc[...].astype(BF16)


def _gather(u, code_e, cum_flat, *, seg_rows):
    n, d = u.shape
    nseg, _, n_tiles, tile = code_e.shape
    cap = EC_CAPACITY * seg_rows // N_EXPERTS
    ts = min(SLOT_TILE, cap)
    return pl.pallas_call(
        functools.partial(_gather_kernel, cap=cap, n_tiles=n_tiles, tile=tile, ts=ts),
        grid_spec=pltpu.PrefetchScalarGridSpec(
            num_scalar_prefetch=1,
            grid=(nseg, N_EXPERTS),
            in_specs=[pl.BlockSpec((seg_rows, d), lambda s, e, *_: (s, 0)),
                      pl.BlockSpec((1, 1, n_tiles, tile), lambda s, e, *_: (s, e, 0, 0))],
            out_specs=pl.BlockSpec((1, 1, cap, d), lambda s, e, *_: (e, s, 0, 0)),
            scratch_shapes=[pltpu.VMEM((cap, d), F32)]),
        out_shape=jax.ShapeDtypeStruct((N_EXPERTS, nseg, cap, d), BF16),
        compiler_params=_cparams(("arbitrary", "arbitrary")),
        name="moe_gather",
    )(cum_flat, u, code_e)


def _ffn_kernel(x_ref, w1_ref, w3_ref, w2_ref, o_ref):
    x = x_ref[0]
    a = _mm(x, w1_ref[0, 0])
    g = _mm(x, w3_ref[0, 0])
    hid = (_silu(a) * g).astype(BF16)
    o_ref[0] = _mm(hid, w2_ref[0, 0]).astype(BF16)


def _ffn(xs, w1, w3, w2, layer):
    e, r, d = xs.shape
    f = w1.shape[3]
    tr = min(512, r)
    return pl.pallas_call(
        _ffn_kernel,
        grid=(e, r // tr),
        in_specs=[pl.BlockSpec((1, tr, d), lambda i, j: (i, j, 0)),
                  pl.BlockSpec((1, 1, d, f), lambda i, j: (layer, i, 0, 0)),
                  pl.BlockSpec((1, 1, d, f), lambda i, j: (layer, i, 0, 0)),
                  pl.BlockSpec((1, 1, f, d), lambda i, j: (layer, i, 0, 0))],
        out_specs=pl.BlockSpec((1, tr, d), lambda i, j: (i, j, 0)),
        out_shape=jax.ShapeDtypeStruct((e, r, d), BF16),
        compiler_params=_cparams(("arbitrary", "arbitrary")),
        name="moe_ffn",
    )(xs, w1, w3, w2)


COMBINE_EXPERTS = 4
COMBINE_SMALL_ROWS = 64
BF16_ROWS = 16


def _window_start(cum_ref, s, e, j, n_tiles, cap, wr):
    lo = cum_ref[(s * N_EXPERTS + e) * (n_tiles + 1) + j]
    return jnp.minimum((lo // BF16_ROWS) * BF16_ROWS, cap - wr)


def _needs_big(cum_ref, s, e, j, n_tiles, cap, ws):
    hi = cum_ref[(s * N_EXPERTS + e) * (n_tiles + 1) + j + 1]
    return hi > _window_start(cum_ref, s, e, j, n_tiles, cap, ws) + ws


def _combine_kernel(cum_ref, *refs, cap, n_tiles, ws, wr):
    two = wr > ws
    small_refs = refs[:COMBINE_EXPERTS]
    big_refs = refs[COMBINE_EXPERTS:2 * COMBINE_EXPERTS] if two else small_refs
    code_ref, gate_ref, x_ref, m5_ref, g_ref, b_ref, o_ref, acc = refs[(2 if two else 1) * COMBINE_EXPERTS:]
    s = pl.program_id(0)
    j = pl.program_id(1)
    eg = pl.program_id(2)

    @pl.when(eg == 0)
    def _():
        acc[...] = jnp.zeros_like(acc)

    tj = code_ref.shape[1]
    lane = lax.broadcasted_iota(I32, (tj, N_EXPERTS), 1)

    def add_rows(e, y_ref, start, rows):
        code = jnp.sum(jnp.where(lane == e, code_ref[0], 0), axis=1, keepdims=True)
        gate = jnp.sum(jnp.where(lane == e, gate_ref[0], 0.0), axis=1, keepdims=True)
        main = min(rows, ROUTE_TILE)
        slot = start + lax.broadcasted_iota(I32, (tj, main), 1)
        acc[...] += _mm(jnp.where(code == slot, gate, 0.0).astype(BF16), y_ref[0:main, :])
        if rows > main:
            slot2 = start + main + lax.broadcasted_iota(I32, (tj, rows - main), 1)
            acc[...] += _mm(jnp.where(code == slot2, gate, 0.0).astype(BF16), y_ref[main:rows, :])

    for k in range(COMBINE_EXPERTS):
        e = eg * COMBINE_EXPERTS + k
        base = (s * N_EXPERTS + e) * (n_tiles + 1)
        any_slot = cum_ref[base + j + 1] > cum_ref[base + j]
        big = _needs_big(cum_ref, s, e, j, n_tiles, cap, ws) if two else False

        @pl.when(any_slot & jnp.logical_not(big))
        def _(e=e, k=k):
            add_rows(e, small_refs[k], _window_start(cum_ref, s, e, j, n_tiles, cap, ws), ws)

        if two:
            @pl.when(big)
            def _(e=e, k=k):
                add_rows(e, big_refs[k], _window_start(cum_ref, s, e, j, n_tiles, cap, wr), wr)

    @pl.when(eg == N_EXPERTS // COMBINE_EXPERTS - 1)
    def _():
        y = DEEPNORM_ALPHA * x_ref[...] + m5_ref[0] * acc[...]
        o_ref[...] = _layer_stats(y) * g_ref[...] + b_ref[...]


def _combine(ys, code_t, gate_t, cum_flat, x, modrows, g, b, *, seg_rows, tile, row0, row_stride):
    n, d = x.shape
    nseg = n // seg_rows
    cap = ys.shape[2]
    tj = tile
    nt = seg_rows // tj
    wr = min(tj + BF16_ROWS, cap)
    ws = min(COMBINE_SMALL_ROWS + BF16_ROWS, cap)
    two = wr > ws

    def window(k, rows, only_if_big):
        def index(s, j, eg, cum_ref):
            e = eg * COMBINE_EXPERTS + k
            row = (e * nseg + s) * cap + _window_start(cum_ref, s, e, j, nt, cap, rows)
            if only_if_big:
                row = jnp.where(_needs_big(cum_ref, s, e, j, nt, cap, ws), row, 0)
            return pl.multiple_of(row, BF16_ROWS), 0
        return pl.BlockSpec((pl.Element(rows), pl.Element(d)), index)

    windows = [window(k, ws, False) for k in range(COMBINE_EXPERTS)]
    if two:
        windows += [window(k, wr, True) for k in range(COMBINE_EXPERTS)]
    y2d = ys.reshape(N_EXPERTS * nseg * cap, d)
    return pl.pallas_call(
        functools.partial(_combine_kernel, cap=cap, n_tiles=nt, ws=ws, wr=wr),
        grid_spec=pltpu.PrefetchScalarGridSpec(
            num_scalar_prefetch=1,
            grid=(nseg, nt, N_EXPERTS // COMBINE_EXPERTS),
            in_specs=windows
            + [pl.BlockSpec((1, tj, N_EXPERTS), lambda s, j, e, *_: (s, j, 0)),
               pl.BlockSpec((1, tj, N_EXPERTS), lambda s, j, e, *_: (s, j, 0)),
               pl.BlockSpec((tj, d), lambda s, j, e, *_: (s * nt + j, 0)),
               pl.BlockSpec((1, 1, d), lambda s, j, e, *_: ((row0 + s * row_stride) * 6 + 5, 0, 0)),
               pl.BlockSpec((1, d), lambda s, j, e, *_: (0, 0)),
               pl.BlockSpec((1, d), lambda s, j, e, *_: (0, 0))],
            out_specs=pl.BlockSpec((tj, d), lambda s, j, e, *_: (s * nt + j, 0)),
            scratch_shapes=[pltpu.VMEM((tj, d), F32)]),
        out_shape=jax.ShapeDtypeStruct((n, d), F32),
        compiler_params=_cparams(("arbitrary", "arbitrary", "arbitrary")),
        name="moe_combine",
    )(cum_flat, *([y2d] * len(windows)), code_t, gate_t, x, modrows, g.reshape(1, d), b.reshape(1, d))


def _moe(x, modrows, w_router, w1, w3, w2, layer, g, b, *, seg_rows, row0, row_stride):
    n, d = x.shape
    nseg = n // seg_rows
    u, code, gate, cum = _router(x, modrows, w_router, seg_rows=seg_rows, row0=row0, row_stride=row_stride)
    tile = code.shape[3]
    code_e = code.transpose(0, 2, 1, 3)
    code_t = code.transpose(0, 1, 3, 2).reshape(nseg, seg_rows, N_EXPERTS)
    gate_t = gate.transpose(0, 1, 3, 2).reshape(nseg, seg_rows, N_EXPERTS)
    cum_e = cum[..., 0].transpose(0, 2, 1)
    cum_x = jnp.pad(cum_e, ((0, 0), (0, 0), (1, 0)))
    cum_flat = cum_x.reshape(-1)
    n_tiles = code.shape[1]
    gm = 2 if n_tiles % 2 == 0 else 1
    xs = _gather(u, code_e.reshape(nseg, N_EXPERTS, n_tiles // gm, tile * gm), cum_x[:, :, ::gm].reshape(-1),
                 seg_rows=seg_rows)
    cap = xs.shape[2]
    ys = _ffn(xs.reshape(N_EXPERTS, nseg * cap, d), w1, w3, w2, layer).reshape(N_EXPERTS, nseg, cap, d)
    return _combine(ys, code_t, gate_t, cum_flat, x, modrows, g, b, seg_rows=seg_rows, tile=tile,
                    row0=row0, row_stride=row_stride)


def kernel(x, c, ctx, c_ctx, w_mod, b_mod, w_in, w_out, ln_g, ln_b, gdn_conv, gdn_a_log, gdn_dt_bias,
           gdn_norm, gla_w_gate, gla_b_gate, gla_norm, hgrn_gamma, hgrn_norm, na_rpb,
           moe_router, moe_w1, moe_w3, moe_w2):
    batch, seq, d = x.shape
    n_ctx = ctx.shape[1]
    xl = x.reshape(batch * seq, d)
    xc = ctx.reshape(batch * n_ctx, d)

    cc = jnp.zeros((8, d), F32).at[:batch].set(c).at[batch].set(c_ctx)
    mods = _modulation(cc, w_mod, b_mod)

    w_in_r = _regroup_w_in(w_in)
    w_out_b = w_out.astype(BF16)
    w1_b, w3_b, w2_b = moe_w1.astype(BF16), moe_w3.astype(BF16), moe_w2.astype(BF16)
    low = jnp.asarray(_LOW_MATS, BF16)

    for l in range(DEPTH):
        keep_ctx = l < DEPTH - 1
        modrows = mods[l].reshape(8 * 6, 1, d)
        p_lat = _inproj(xl, modrows, w_in_r, l, seg_rows=seq, row0=0)
        p_ctx = _inproj(xc, modrows, w_in_r, l, seg_rows=batch * n_ctx, row0=batch)
        mixes = [
            _gdn(p_lat, p_ctx, gdn_conv[l], gdn_a_log[l], gdn_dt_bias[l], gdn_norm[l], batch=batch),
            _gla(p_lat, p_ctx, gla_w_gate[l], gla_b_gate[l], gla_norm[l], low, batch=batch),
            _hgrn(p_lat, p_ctx, hgrn_gamma, hgrn_norm[l], low, batch=batch, layer=l),
            _na(p_lat, p_ctx, na_rpb[l], batch=batch),
        ]
        xl = _outproj([m[0] for m in mixes], w_out_b, l, xl, modrows, ln_g[l, 0], ln_b[l, 0],
                      seg_rows=seq, row0=0)
        if keep_ctx:
            xc = _outproj([m[1] for m in mixes], w_out_b, l, xc, modrows, ln_g[l, 0], ln_b[l, 0],
                          seg_rows=batch * n_ctx, row0=batch)
        moe_args = (moe_router[l], w1_b, w3_b, w2_b, l, ln_g[l, 1], ln_b[l, 1])
        xl = _moe(xl, modrows, *moe_args, seg_rows=seq, row0=0, row_stride=1)
        if keep_ctx:
            xc = _moe(xc, modrows, *moe_args, seg_rows=n_ctx, row0=batch, row_stride=0)
    return xl.reshape(batch, seq, d)
```

```python
import functools

import numpy as np
import jax
import jax.numpy as jnp
from jax import lax
from jax.experimental import pallas as pl
from jax.experimental.pallas import tpu as pltpu

F32 = jnp.float32
BF16 = jnp.bfloat16
I32 = jnp.int32

D_MODEL = 2048
DEPTH = 4
GRID_W = 64
HEAD_DIM = 128
N_GROUPS = 4
GROUP_WIDTH = D_MODEL // N_GROUPS
GROUP_HEADS = GROUP_WIDTH // HEAD_DIM
GDN_CONV = 5
GDN_CHUNK = 64
ROWS = 256
GLA_DK = HEAD_DIM // 2
GLA_RANK = 16
GLA_TAU = 16.0
NA_ROWS = 8
NA_COLS = 16
ROPE_BASE = 10000.0
N_EXPERTS = 16
EC_CAPACITY = 2
D_EXPERT = D_MODEL // 2
DEEPNORM_ALPHA = (2 * DEPTH) ** 0.25
LN_EPS = 1e-5
NORM_EPS = 1e-6
NEG_BIG = -1e30

ROUTE_TILE = 256
LANES = 128
SUBLANES = 8
VMEM_LIMIT = 56 * 1024 * 1024

_OFF = {}
_o = 0
for _name, _w in (("a_q", 512), ("a_k", 512), ("a_v", 512), ("a_z", 512), ("a_beta", 8), ("a_dec", 8),
                  ("b_q", 256), ("b_k", 256), ("b_v", 512), ("b_r", 512), ("b_code", 32),
                  ("c_q", 512), ("c_i", 512), ("c_g", 512), ("c_f", 1024),
                  ("d_q", 512), ("d_k", 512), ("d_v", 512)):
    _OFF[_name] = _o
    _o += _w
N_IN = _o

BLK = dict(a_q=0, a_k=4, a_v=8, a_z=12, c_q=16, c_i=20, c_g=24, c_f0=28, c_f1=32,
           d_q=36, d_k=40, d_v=44, b_v=48, b_r=52, b_q=56, b_k=60, small=64)
NP_BLOCKS = 66
NP = NP_BLOCKS * LANES
INPROJ_TN = NP // 3
SM_BETA, SM_DEC, SM_CODE = 0, 8, 16


def _regroup_w_in(w_in, dtype=BF16):
    def cols(a, b):
        return w_in[..., a:b]

    def heads_padded(name):
        w = cols(_OFF[name], _OFF[name] + GROUP_HEADS * GLA_DK)
        w = w.reshape(w.shape[:-1] + (GROUP_HEADS, GLA_DK))
        w = jnp.pad(w, [(0, 0)] * (w.ndim - 1) + [(0, LANES - GLA_DK)])
        return w.reshape(w.shape[:-2] + (GROUP_HEADS * LANES,))

    small = jnp.concatenate([cols(_OFF["a_beta"], _OFF["a_beta"] + 16), cols(_OFF["b_code"], _OFF["b_code"] + 32)],
                            axis=-1)
    small = jnp.pad(small, [(0, 0)] * (small.ndim - 1) + [(0, (NP_BLOCKS - BLK["small"]) * LANES - 48)])
    parts = [cols(_OFF["a_q"], _OFF["a_q"] + 2048),
             cols(_OFF["c_q"], _OFF["c_q"] + 2560),
             cols(_OFF["d_q"], _OFF["d_q"] + 1536),
             cols(_OFF["b_v"], _OFF["b_v"] + 1024),
             heads_padded("b_q"), heads_padded("b_k"), small]
    return jnp.concatenate(parts, axis=-1).astype(dtype)


def _cparams(sem):
    return pltpu.CompilerParams(dimension_semantics=sem, vmem_limit_bytes=VMEM_LIMIT)


def _sigmoid(x):
    return 1.0 / (1.0 + jnp.exp(-x))


def _silu(x):
    return x * _sigmoid(x)


def _split2(x):
    hi = x.astype(BF16)
    lo = (x - hi.astype(F32)).astype(BF16)
    return hi, lo


def _split3(x):
    hi = x.astype(BF16)
    r = x - hi.astype(F32)
    mid = r.astype(BF16)
    lo = (r - mid.astype(F32)).astype(BF16)
    return hi, mid, lo


def _mm(a, b):
    return jnp.dot(a, b, preferred_element_type=F32)


def _mm_nt(a, b):
    return lax.dot_general(a, b, (((1,), (1,)), ((), ())), preferred_element_type=F32)


def _mm_tn(a, b):
    return lax.dot_general(a, b, (((0,), (0,)), ((), ())), preferred_element_type=F32)


def _mm_x2(a, b):
    ah, al = _split2(a)
    bh, bl = _split2(b)
    return _mm(ah, bh) + _mm(ah, bl) + _mm(al, bh)


def _mm_exact_lhs(a_bf, b):
    bh, bm, bl = _split3(b)
    return _mm(a_bf, bh) + _mm(a_bf, bm) + _mm(a_bf, bl)


def _mm_exact_rhs(a, b_bf):
    ah, am, al = _split3(a)
    return _mm(ah, b_bf) + _mm(am, b_bf) + _mm(al, b_bf)


def _layer_stats(x):
    mu = jnp.mean(x, axis=-1, keepdims=True)
    xc = x - mu
    var = jnp.mean(xc * xc, axis=-1, keepdims=True)
    return xc * lax.rsqrt(var + LN_EPS)


def _mod_kernel(c_ref, w_ref, b_ref, o_ref):
    s = _silu(c_ref[...])
    hi, lo = _split2(s)
    w = w_ref[0].astype(BF16)
    o_ref[0] = _mm(hi, w) + _mm(lo, w) + b_ref[0]


def _modulation(cc, w_mod, b_mod):
    depth, d, n6 = w_mod.shape
    tn = 1024
    return pl.pallas_call(
        _mod_kernel,
        grid=(depth, n6 // tn),
        in_specs=[pl.BlockSpec((8, d), lambda l, j: (0, 0)),
                  pl.BlockSpec((1, d, tn), lambda l, j: (l, 0, j)),
                  pl.BlockSpec((1, 1, tn), lambda l, j: (l, 0, j))],
        out_specs=pl.BlockSpec((1, 8, tn), lambda l, j: (l, 0, j)),
        out_shape=jax.ShapeDtypeStruct((depth, 8, n6), F32),
        compiler_params=_cparams(("arbitrary", "arbitrary")),
        name="modulation",
    )(cc, w_mod, b_mod.reshape(depth, 1, n6))


def _inproj_kernel(x_ref, sh_ref, sc_ref, w_ref, o_ref, u_scr):
    @pl.when(pl.program_id(1) == 0)
    def _():
        y = _layer_stats(x_ref[...]) * (1.0 + sc_ref[0]) + sh_ref[0]
        u_scr[...] = y.astype(BF16)

    o_ref[...] = _mm(u_scr[...], w_ref[0])


def _inproj(x, modrows, w, layer, *, seg_rows, row0):
    n, d = x.shape
    tm = min(512, n, seg_rows)
    tn = INPROJ_TN
    tiles_per_seg = seg_rows // tm

    def mrow(k):
        return lambda i, j: ((row0 + i // tiles_per_seg) * 6 + k, 0, 0)

    return pl.pallas_call(
        _inproj_kernel,
        grid=(n // tm, NP // tn),
        in_specs=[pl.BlockSpec((tm, d), lambda i, j: (i, 0)),
                  pl.BlockSpec((1, 1, d), mrow(0)),
                  pl.BlockSpec((1, 1, d), mrow(1)),
                  pl.BlockSpec((1, d, tn), lambda i, j: (layer, 0, j))],
        out_specs=pl.BlockSpec((tm, tn), lambda i, j: (i, j)),
        out_shape=jax.ShapeDtypeStruct((n, NP), F32),
        scratch_shapes=[pltpu.VMEM((tm, d), BF16)],
        compiler_params=_cparams(("arbitrary", "arbitrary")),
        name="inproj",
    )(x, modrows, modrows, w)


def _outproj_kernel(m0, m1, m2, m3, w_ref, x_ref, gate_ref, g_ref, b_ref, o_ref):
    gw = GROUP_WIDTH
    acc = _mm(m0[...], w_ref[0, 0 * gw:1 * gw, :])
    acc += _mm(m1[...], w_ref[0, 1 * gw:2 * gw, :])
    acc += _mm(m2[...], w_ref[0, 2 * gw:3 * gw, :])
    acc += _mm(m3[...], w_ref[0, 3 * gw:4 * gw, :])
    y = DEEPNORM_ALPHA * x_ref[...] + gate_ref[0] * acc
    o_ref[...] = _layer_stats(y) * g_ref[...] + b_ref[...]


def _outproj(mixes, w, layer, x, modrows, g, b, *, seg_rows, row0):
    n, d = x.shape
    tm = min(512, n, seg_rows)
    tiles_per_seg = seg_rows // tm
    mspec = pl.BlockSpec((tm, GROUP_WIDTH), lambda i: (i, 0))
    return pl.pallas_call(
        _outproj_kernel,
        grid=(n // tm,),
        in_specs=[mspec, mspec, mspec, mspec,
                  pl.BlockSpec((1, d, d), lambda i: (layer, 0, 0)),
                  pl.BlockSpec((tm, d), lambda i: (i, 0)),
                  pl.BlockSpec((1, 1, d), lambda i: ((row0 + i // tiles_per_seg) * 6 + 2, 0, 0)),
                  pl.BlockSpec((1, d), lambda i: (0, 0)),
                  pl.BlockSpec((1, d), lambda i: (0, 0))],
        out_specs=pl.BlockSpec((tm, d), lambda i: (i, 0)),
        out_shape=jax.ShapeDtypeStruct((n, d), F32),
        compiler_params=_cparams(("arbitrary",)),
        name="outproj",
    )(*mixes, w, x, modrows, g.reshape(1, d), b.reshape(1, d))


def _bwd_chunk_index(i, n_ctx_chunks, n_chunks):
    return jnp.where(i < n_ctx_chunks, n_ctx_chunks - 1 - i, n_chunks + n_ctx_chunks - 1 - i)


def _rms_gate_store(of_s, ob_s, g_ref, gate_fn, gate_refs, out_refs, seg_bounds):
    for (lo, hi), gate_ref, out_ref in zip(seg_bounds, gate_refs, out_refs):
        n = hi - lo
        tile = min(512, n)
        for r0 in range(0, n, tile):
            o = of_s[lo + r0: lo + r0 + tile, :] + ob_s[lo + r0: lo + r0 + tile, :]
            y = o * lax.rsqrt(jnp.mean(o * o, axis=-1, keepdims=True) + NORM_EPS) * g_ref[...]
            out_ref[r0:r0 + tile, :] = (y * gate_fn(gate_ref[r0:r0 + tile, :])).astype(out_ref.dtype)


def _select_col(x, lane, c):
    return jnp.sum(jnp.where(lane == c, x, 0.0), axis=1, keepdims=True)


def _rows8(row):
    return jnp.broadcast_to(row, (SUBLANES, LANES))


def _gdn_kernel(ql, kl, vl, zl, sl, qc, kc, vc, zc, sc, wq, wk, wv, alog, dtb, gn,
                ol, oc, q_s, k_s, v_s, tok_s, pad_s, u_s, w_s, qg_s, kd_s, p_s, ge_s, *, n_ctx, n_lat):
    h = pl.program_id(1)
    n_tot = n_ctx + n_lat
    segs = ((0, n_ctx), (n_ctx, n_tot))

    def conv_into(x_ref, w_ref, dst, lo, n, l2_scale):
        pad_s[0:8, :] = jnp.zeros((8, LANES), F32)
        pad_s[8:8 + n, :] = x_ref[...]
        pad_s[8 + n:16 + n, :] = jnp.zeros((8, LANES), F32)
        tile = min(512, n)
        for r0 in range(0, n, tile):
            acc = jnp.zeros((tile, LANES), F32)
            for i in range(GDN_CONV):
                s0 = 8 + r0 + i - GDN_CONV // 2
                acc = acc + pad_s[s0:s0 + tile, :] * w_ref[i:i + 1, :]
            y = _silu(acc)
            if l2_scale is not None:
                y = y * (lax.rsqrt(jnp.sum(y * y, axis=-1, keepdims=True) + NORM_EPS) * l2_scale)
            dst[lo + r0: lo + r0 + tile, :] = y

    for (lo, hi), (xq, xk, xv) in zip(segs, ((qc, kc, vc), (ql, kl, vl))):
        conv_into(xq, wq, q_s, lo, hi - lo, HEAD_DIM ** -0.5)
        conv_into(xk, wk, k_s, lo, hi - lo, 1.0)
        conv_into(xv, wv, v_s, lo, hi - lo, None)

    def tok_into(s_ref, lo, n):
        tile = min(512, n)
        lane = lax.broadcasted_iota(I32, (tile, LANES), 1)
        for r0 in range(0, n, tile):
            sm = s_ref[r0:r0 + tile, :]
            cols = []
            for d in range(2):
                beta = _sigmoid(_select_col(sm, lane, SM_BETA + d * GROUP_HEADS + h))
                dec = _select_col(sm, lane, SM_DEC + d * GROUP_HEADS + h)
                a_neg = -jnp.exp(_select_col(alog[...], lane[0:1], d * GROUP_HEADS + h))
                bias = _select_col(dtb[...], lane[0:1], d * GROUP_HEADS + h)
                la = a_neg * jax.nn.softplus(dec + bias)
                cols += [beta, la]
            t = jnp.where(lane == 0, cols[0], jnp.where(lane == 1, cols[1],
                          jnp.where(lane == 2, cols[2], jnp.where(lane == 3, cols[3], 0.0))))
            tok_s[lo + r0: lo + r0 + tile, :] = t

    tok_into(sc, 0, n_ctx)
    tok_into(sl, n_ctx, n_lat)

    c = GDN_CHUNK
    nb = ROWS // c
    r = lax.broadcasted_iota(I32, (ROWS, ROWS), 0)
    cc = lax.broadcasted_iota(I32, (ROWS, ROWS), 1)
    same = (r // c) == (cc // c)
    same_bf = same.astype(BF16)
    eye = r == cc
    eye_f = eye.astype(F32)

    def phase1(si, _):
        off = pl.multiple_of(si * ROWS, ROWS)
        q = q_s[pl.ds(off, ROWS), :]
        k = k_s[pl.ds(off, ROWS), :]
        v = v_s[pl.ds(off, ROWS), :]
        tk = tok_s[pl.ds(off, ROWS), :]
        qb = q.astype(BF16)
        kb = k.astype(BF16)
        kk = _mm_nt(kb, kb)
        qk = _mm_nt(qb, kb)
        dirs = (0, 1)
        beta = [tk[:, 2 * d:2 * d + 1] for d in dirs]
        incl = [same & ((cc >= r) if d else (cc <= r)) for d in dirs]
        strict = [same & ((cc > r) if d else (cc < r)) for d in dirs]
        gi_full = [_mm_exact_lhs(incl[d].astype(BF16), jnp.broadcast_to(tk[:, 2 * d + 1:2 * d + 2], (ROWS, LANES)))
                   for d in dirs]
        gi = [jnp.concatenate([gi_full[d], gi_full[d]], axis=1) for d in dirs]
        gj = [_mm_exact_lhs(same_bf, jnp.where(eye, gi[d], 0.0)) for d in dirs]
        diff = [gi[d] - gj[d] for d in dirs]
        a = [beta[d] * kk * jnp.exp(jnp.where(strict[d], diff[d], -jnp.inf)) for d in dirs]
        t = [eye_f - a[d] for d in dirs]
        pb = [a[d].astype(BF16) for d in dirs]
        for _ in range(5):
            pb = [_mm(pb[d], pb[d]).astype(BF16) for d in dirs]
            t = [t[d] + _mm(t[d].astype(BF16), pb[d]) for d in dirs]
        eg = [jnp.exp(gi_full[d]) for d in dirs]
        sol = [_mm_x2(t[d], jnp.concatenate([beta[d] * v, (beta[d] * eg[d]) * k], axis=1)) for d in dirs]
        for d in dirs:
            p = qk * jnp.exp(jnp.where(incl[d], diff[d], -jnp.inf))
            for kb_ in range(nb):
                p_s[d, pl.ds(pl.multiple_of(off + kb_ * c, c), c), :] = (
                    p[kb_ * c:(kb_ + 1) * c, kb_ * c:(kb_ + 1) * c].astype(BF16))
            u_s[d, pl.ds(off, ROWS), :] = sol[d][:, :LANES]
            w_s[d, pl.ds(off, ROWS), :] = sol[d][:, LANES:].astype(BF16)
            gr = gi_full[d].reshape(nb, c, LANES)
            gl = gr[:, 0:1, :] if d else gr[:, c - 1:c, :]
            g_last = jnp.broadcast_to(gl, (nb, c, LANES)).reshape(ROWS, LANES)
            qg_s[d, pl.ds(off, ROWS), :] = (q * eg[d]).astype(BF16)
            kd_s[d, pl.ds(off, ROWS), :] = (k * jnp.exp(g_last - gi_full[d])).astype(BF16)
            ge = jnp.broadcast_to(jnp.exp(gl), (nb, SUBLANES, LANES)).reshape(nb * SUBLANES, LANES)
            ge_s[d, pl.ds(pl.multiple_of(si * nb * SUBLANES, nb * SUBLANES), nb * SUBLANES), :] = ge
        return 0

    lax.fori_loop(0, n_tot // ROWS, phase1, 0)

    of_s, ob_s = q_s, k_s
    n_cc = n_ctx // c
    n_ch = n_tot // c

    def step(i, carry):
        dirs = (0, 1)
        chs = (i, _bwd_chunk_index(i, n_cc, n_ch))
        offs = [pl.multiple_of(chs[d] * c, c) for d in dirs]
        ge = [ge_s[d, pl.ds(pl.multiple_of(chs[d] * SUBLANES, SUBLANES), SUBLANES), :][0:1, :] for d in dirs]
        wq = [jnp.concatenate([w_s[d, pl.ds(offs[d], c), :], qg_s[d, pl.ds(offs[d], c), :]], axis=0) for d in dirs]
        r1 = [_mm(wq[d], carry[d].astype(BF16)) for d in dirs]
        vb = [(u_s[d, pl.ds(offs[d], c), :] - r1[d][:c]).astype(BF16) for d in dirs]
        upd = [_mm_tn(kd_s[d, pl.ds(offs[d], c), :], vb[d]) for d in dirs]
        o = [r1[d][c:] + _mm(p_s[d, pl.ds(offs[d], c), :], vb[d]) for d in dirs]
        of_s[pl.ds(offs[0], c), :] = o[0]
        ob_s[pl.ds(offs[1], c), :] = o[1]
        return tuple(ge[d] * carry[d] + upd[d] for d in dirs)

    z0 = jnp.zeros((HEAD_DIM, HEAD_DIM), F32)
    lax.fori_loop(0, n_ch, step, (z0, z0))

    _rms_gate_store(of_s, ob_s, gn, _silu, (zc, zl), (oc, ol), segs)


def _gdn(p_lat, p_ctx, conv_w, a_log, dt_bias, gnorm, *, batch):
    n_lat = p_lat.shape[0] // batch
    n_ctx = p_ctx.shape[0] // batch
    n_tot = n_lat + n_ctx
    assert n_lat % ROWS == 0 and n_ctx % ROWS == 0
    n_ch = n_tot // GDN_CHUNK

    def blk(n, base):
        return pl.BlockSpec((n, LANES), lambda b, h: (b, base + h))

    def small(n):
        return pl.BlockSpec((n, LANES), lambda b, h: (b, BLK["small"]))

    def cw(base):
        return pl.BlockSpec((8, LANES), lambda b, h: (0, base + h))

    row = pl.BlockSpec((1, LANES), lambda b, h: (0, 0))
    conv_p = jnp.zeros((8, 3 * GROUP_WIDTH), F32).at[:GDN_CONV].set(conv_w)
    alog_row = jnp.zeros((1, LANES), F32).at[0, :2 * GROUP_HEADS].set(a_log.reshape(-1))
    dtb_row = jnp.zeros((1, LANES), F32).at[0, :2 * GROUP_HEADS].set(dt_bias.reshape(-1))
    lat_in = [blk(n_lat, BLK[k]) for k in ("a_q", "a_k", "a_v", "a_z")] + [small(n_lat)]
    ctx_in = [blk(n_ctx, BLK[k]) for k in ("a_q", "a_k", "a_v", "a_z")] + [small(n_ctx)]
    kern = functools.partial(_gdn_kernel, n_ctx=n_ctx, n_lat=n_lat)
    return pl.pallas_call(
        kern,
        grid=(batch, GROUP_HEADS),
        in_specs=lat_in + ctx_in + [cw(0), cw(4), cw(8), row, row, row],
        out_specs=[pl.BlockSpec((n_lat, LANES), lambda b, h: (b, h)),
                   pl.BlockSpec((n_ctx, LANES), lambda b, h: (b, h))],
        out_shape=[jax.ShapeDtypeStruct((batch * n_lat, GROUP_WIDTH), BF16),
                   jax.ShapeDtypeStruct((batch * n_ctx, GROUP_WIDTH), BF16)],
        scratch_shapes=[pltpu.VMEM((n_tot, LANES), F32)] * 4
        + [pltpu.VMEM((n_lat + 16, LANES), F32),
           pltpu.VMEM((2, n_tot, LANES), F32),
           pltpu.VMEM((2, n_tot, LANES), BF16),
           pltpu.VMEM((2, n_tot, LANES), BF16),
           pltpu.VMEM((2, n_tot, LANES), BF16),
           pltpu.VMEM((2, n_tot, GDN_CHUNK), BF16),
           pltpu.VMEM((2, n_ch * SUBLANES, LANES), F32)],
        compiler_params=_cparams(("arbitrary", "arbitrary")),
        name="gdn",
    )(*([p_lat] * 5), *([p_ctx] * 5), conv_p, conv_p, conv_p, alog_row, dtb_row, gnorm.reshape(1, LANES))


DROWS = 128
N_LEVELS = 7


def _cumsum_matrices():
    i = np.arange(DROWS)[:, None]
    u = np.arange(DROWS)[None, :]
    return np.stack([(u <= i), (u >= i)]).astype(np.float32)


_LOW_MATS = _cumsum_matrices()


def _level_index(rev):
    r = lax.broadcasted_iota(I32, (DROWS, DROWS), 0)
    cc = lax.broadcasted_iota(I32, (DROWS, DROWS), 1)
    lev = 31 - lax.clz(r ^ cc)
    earlier = (cc > r) if rev else (cc < r)
    return jnp.where(earlier, lev, jnp.where(r == cc, -1, -2))


def _rows_at(b, rowi, m, shifts, cache):
    c = b.shape[0]
    out = None
    for mval, sh in enumerate(shifts):
        if sh not in cache:
            cache[sh] = b if sh == 0 else pltpu.roll(b, sh % c, 0)
        out = cache[sh] if out is None else jnp.where(m == mval, cache[sh], out)
    src = rowi - sum(jnp.where(m == mval, sh, 0) for mval, sh in enumerate(shifts))
    return jnp.where((src >= 0) & (src < c), out, 0.0)


def _block_sums(b, rev, rowi):
    c = DROWS
    win, wout, rolled = {}, {}, {}
    for lv in (1, 2):
        s = 1 << lv
        m = rowi & (s - 1)
        if rev:
            win[lv] = b - _rows_at(b, rowi, m, [-(s - t) for t in range(s)], rolled)
            wout[lv] = _rows_at(b, rowi, m, list(range(s)), rolled) - b
        else:
            win[lv] = b - _rows_at(b, rowi, m, [t + 1 for t in range(s)], rolled)
            wout[lv] = _rows_at(b, rowi, m, [-(s - 1 - t) for t in range(s)], rolled) - b
    for lv in range(3, N_LEVELS + 1):
        s = 1 << lv
        nblk = c // s
        br = b.reshape(nblk, s, LANES)
        edge = br[:, 0:1, :] if rev else br[:, s - 1:s, :]
        zero = jnp.zeros((1, 1, LANES), F32)
        if nblk == 1:
            before = zero
        elif rev:
            before = jnp.concatenate([edge[1:], zero], axis=0)
        else:
            before = jnp.concatenate([zero, edge[:-1]], axis=0)
        win[lv] = (br - before).reshape(c, LANES)
        wout[lv] = (edge - br).reshape(c, LANES)
    return win, wout


def _diag_phase1(probs, low_ref, levs):
    c = DROWS
    idx = range(len(probs))
    rowi = lax.broadcasted_iota(I32, (c, LANES), 0)
    bs = [_mm_exact_lhs(low_ref[1 if p[4] else 0], p[3]) for p in probs]
    sums = [_block_sums(bs[i], probs[i][4], rowi) for i in idx]
    qb = [p[0].astype(BF16) for p in probs]
    kb = [p[1].astype(BF16) for p in probs]
    lev = [levs[1 if p[4] else 0] for p in probs]
    pm = [jnp.where(lev[i] == -1, _mm_nt(qb[i], kb[i]), 0.0) for i in idx]
    for lv in range(N_LEVELS):
        qs = [(probs[i][0] * jnp.exp(probs[i][3] if lv == 0 else sums[i][0][lv])).astype(BF16) for i in idx]
        ks = [kb[i] if lv == 0 else (probs[i][1] * jnp.exp(sums[i][1][lv])).astype(BF16) for i in idx]
        dots = [_mm_nt(qs[i], ks[i]) for i in idx]
        pm = [jnp.where(lev[i] == lv, dots[i], pm[i]) for i in idx]
    o_intra = [_mm(pm[i].astype(BF16), probs[i][2]) for i in idx]
    qg = [(probs[i][0] * jnp.exp(sums[i][0][N_LEVELS])).astype(BF16) for i in idx]
    kd = [(probs[i][1] * jnp.exp(sums[i][1][N_LEVELS])).astype(BF16) for i in idx]
    a_end = [jnp.exp(bs[i][0:1, :] if probs[i][4] else bs[i][c - 1:c, :]) for i in idx]
    return [(o_intra[i], qg[i], kd[i], a_end[i]) for i in idx]


DIAG_CHUNKS_PER_STEP = 2


def _diag_run_phase1(load, lo, n_rows, low_ref, levs, v_bs, o_s, qg_s, kd_s, ae_s):
    def step(pi, _):
        probs, offs = [], []
        for u in range(DIAG_CHUNKS_PER_STEP):
            r0 = pl.multiple_of((pi * DIAG_CHUNKS_PER_STEP + u) * DROWS, DROWS)
            q, kf, kb, v, laf, lab = load(r0)
            off = pl.multiple_of(lo + r0, DROWS)
            v_bf = v.astype(BF16)
            v_bs[pl.ds(off, DROWS), :] = v_bf
            probs += [(q, kf, v_bf, laf, False), (q, kb, v_bf, lab, True)]
            offs += [(off, 0), (off, 1)]
        for (off, d), (o_intra, qg, kd, a_end) in zip(offs, _diag_phase1(probs, low_ref, levs)):
            o_s[d][pl.ds(off, DROWS), :] = o_intra
            qg_s[d, pl.ds(off, DROWS), :] = qg
            kd_s[d, pl.ds(off, DROWS), :] = kd
            ci = off // DROWS
            ae_s[d, pl.ds(pl.multiple_of(ci * SUBLANES, SUBLANES), SUBLANES), :] = _rows8(a_end)
        return 0

    lax.fori_loop(0, n_rows // (DIAG_CHUNKS_PER_STEP * DROWS), step, 0)


def _diag_phase2(v_bs, o_s, qg_s, kd_s, ae_s, n_ctx, n_tot):
    n_cc = n_ctx // DROWS
    n_ch = n_tot // DROWS

    def step(i, carry):
        dirs = (0, 1)
        chs = (i, _bwd_chunk_index(i, n_cc, n_ch))
        offs = [pl.multiple_of(chs[d] * DROWS, DROWS) for d in dirs]
        a_end = [ae_s[d, pl.ds(pl.multiple_of(chs[d] * SUBLANES, SUBLANES), SUBLANES), :][0:1, :] for d in dirs]
        stb = [carry[d].astype(BF16) for d in dirs]
        inter = [_mm_nt(qg_s[d, pl.ds(offs[d], DROWS), :], stb[d]) for d in dirs]
        upd = [_mm_tn(v_bs[pl.ds(offs[d], DROWS), :], kd_s[d, pl.ds(offs[d], DROWS), :]) for d in dirs]
        for d in dirs:
            o_s[d][pl.ds(offs[d], DROWS), :] += inter[d]
        return tuple(carry[d] * a_end[d] + upd[d] for d in dirs)

    z0 = jnp.zeros((HEAD_DIM, HEAD_DIM), F32)
    lax.fori_loop(0, n_ch, step, (z0, z0))


def _diag_scratch(n_tot):
    n_ch = n_tot // DROWS
    return [pltpu.VMEM((n_tot, LANES), BF16),
            pltpu.VMEM((n_tot, LANES), F32),
            pltpu.VMEM((n_tot, LANES), F32),
            pltpu.VMEM((2, n_tot, LANES), BF16),
            pltpu.VMEM((2, n_tot, LANES), BF16),
            pltpu.VMEM((2, n_ch * SUBLANES, LANES), F32)]


def _gla_kernel(ql, kl, vl, rl, sl, qc, kc, vc, rc, sc, cos_ref, sin_ref, perm_ref, wg_ref, bg_ref, gn,
                low_ref, ol, oc, v_bs, of_s, ob_s, qg_s, kd_s, ae_s, *, n_ctx, n_lat):
    n_tot = n_ctx + n_lat
    segs = ((0, n_ctx), (n_ctx, n_tot))
    levs = (_level_index(False), _level_index(True))

    for (lo, hi), (xq, xk, xv, xs), rope in zip(segs, ((qc, kc, vc, sc), (ql, kl, vl, sl)), (False, True)):
        def load(r0, xq=xq, xk=xk, xv=xv, xs=xs, rope=rope):
            q = xq[pl.ds(r0, DROWS), :]
            k = xk[pl.ds(r0, DROWS), :]
            if rope:
                cs = cos_ref[pl.ds(r0, DROWS), :]
                sn = sin_ref[pl.ds(r0, DROWS), :]
                q = q * cs + _mm_exact_rhs(q, perm_ref[...]) * sn
                k = k * cs + _mm_exact_rhs(k, perm_ref[...]) * sn
            sm = xs[pl.ds(r0, DROWS), :]
            las = [jax.nn.log_sigmoid(_mm_x2(sm, wg_ref[d, 0]) + bg_ref[d, 0]) * (1.0 / GLA_TAU) for d in range(2)]
            return q * (GLA_DK ** -0.5), k, k, xv[pl.ds(r0, DROWS), :], las[0], las[1]

        _diag_run_phase1(load, lo, hi - lo, low_ref, levs, v_bs, (of_s, ob_s), qg_s, kd_s, ae_s)

    _diag_phase2(v_bs, (of_s, ob_s), qg_s, kd_s, ae_s, n_ctx, n_tot)
    _rms_gate_store(of_s, ob_s, gn, _silu, (rc, rl), (oc, ol), segs)


def _rope_tables(n_lat):
    t = np.arange(n_lat)
    nf = GLA_DK // 4
    inv = (ROPE_BASE ** (-jnp.arange(nf, dtype=F32) / nf))
    lane = np.arange(LANES)
    f = lane % nf
    use_col = (lane % GLA_DK) >= GLA_DK // 2
    first = (lane % (2 * nf)) < nf
    real = lane < GLA_DK
    pos = jnp.where(use_col[None, :], (t % GRID_W)[:, None], (t // GRID_W)[:, None]).astype(F32)
    ang = pos * inv[f][None, :]
    cos = jnp.where(real[None, :], jnp.cos(ang), 1.0)
    sin = jnp.where(real[None, :], jnp.where(first[None, :], -jnp.sin(ang), jnp.sin(ang)), 0.0)
    partner = np.where(first, lane + nf, lane - nf)
    perm = np.zeros((LANES, LANES), np.float32)
    perm[partner[real], lane[real]] = 1.0
    return cos.astype(F32), sin.astype(F32), jnp.asarray(perm, BF16)


def _gla(p_lat, p_ctx, w_gate, b_gate, gnorm, low, *, batch):
    n_lat = p_lat.shape[0] // batch
    n_ctx = p_ctx.shape[0] // batch
    n_tot = n_lat + n_ctx
    assert n_lat % DROWS == 0 and n_ctx % DROWS == 0
    cos, sin, perm = _rope_tables(n_lat)
    wg = jnp.zeros((2, GROUP_HEADS, LANES, LANES), F32)
    wsrc = w_gate.reshape(2, GLA_RANK, GROUP_HEADS, GLA_DK).transpose(0, 2, 1, 3)
    for d in range(2):
        wg = wg.at[d, :, SM_CODE + d * GLA_RANK: SM_CODE + (d + 1) * GLA_RANK, :GLA_DK].set(wsrc[d])
    bg = jnp.zeros((2, GROUP_HEADS, 1, LANES), F32).at[:, :, 0, :GLA_DK].set(
        b_gate.reshape(2, GROUP_HEADS, GLA_DK))

    def blk(n, base):
        return pl.BlockSpec((n, LANES), lambda b, h: (b, base + h))

    def small(n):
        return pl.BlockSpec((n, LANES), lambda b, h: (b, BLK["small"]))

    def const2(shape):
        return pl.BlockSpec(shape, lambda b, h: (0, 0))

    names = ("b_q", "b_k", "b_v", "b_r")
    kern = functools.partial(_gla_kernel, n_ctx=n_ctx, n_lat=n_lat)
    return pl.pallas_call(
        kern,
        grid=(batch, GROUP_HEADS),
        in_specs=[blk(n_lat, BLK[k]) for k in names] + [small(n_lat)]
        + [blk(n_ctx, BLK[k]) for k in names] + [small(n_ctx)]
        + [const2((n_lat, LANES)), const2((n_lat, LANES)), const2((LANES, LANES)),
           pl.BlockSpec((2, 1, LANES, LANES), lambda b, h: (0, h, 0, 0)),
           pl.BlockSpec((2, 1, 1, LANES), lambda b, h: (0, h, 0, 0)),
           const2((1, LANES)),
           pl.BlockSpec(low.shape, lambda b, h: (0, 0, 0))],
        out_specs=[pl.BlockSpec((n_lat, LANES), lambda b, h: (b, h)),
                   pl.BlockSpec((n_ctx, LANES), lambda b, h: (b, h))],
        out_shape=[jax.ShapeDtypeStruct((batch * n_lat, GROUP_WIDTH), BF16),
                   jax.ShapeDtypeStruct((batch * n_ctx, GROUP_WIDTH), BF16)],
        scratch_shapes=_diag_scratch(n_tot),
        compiler_params=_cparams(("arbitrary", "arbitrary")),
        name="gla",
    )(*([p_lat] * 5), *([p_ctx] * 5), cos, sin, perm, wg, bg, gnorm.reshape(1, LANES), low)


def _hgrn_kernel(ql, il, gl, f0l, f1l, qc, ic, gc, f0c, f1c, gam_ref, gn, low_ref,
                 ol, oc, v_bs, of_s, ob_s, qg_s, kd_s, ae_s, *, n_ctx, n_lat, layer):
    n_tot = n_ctx + n_lat
    segs = ((0, n_ctx), (n_ctx, n_tot))
    levs = (_level_index(False), _level_index(True))
    lbs = []
    for d in range(2):
        gam = gam_ref[d]
        ex = jnp.exp(gam - jnp.max(gam, axis=0, keepdims=True))
        pr = ex / jnp.sum(ex, axis=0, keepdims=True)
        lb = jnp.zeros((1, LANES), F32)
        for m in range(1, layer + 1):
            lb = lb + pr[m:m + 1, :]
        lbs.append(lb)

    for (lo, hi), (xq, xi, xf0, xf1) in zip(segs, ((qc, ic, f0c, f1c), (ql, il, f0l, f1l))):
        def load(r0, xq=xq, xi=xi, xf0=xf0, xf1=xf1):
            fs = [lbs[d] + (1.0 - lbs[d]) * _sigmoid(xf[pl.ds(r0, DROWS), :]) for d, xf in enumerate((xf0, xf1))]
            return (xq[pl.ds(r0, DROWS), :], 1.0 - fs[0], 1.0 - fs[1], xi[pl.ds(r0, DROWS), :],
                    jnp.log(fs[0]), jnp.log(fs[1]))

        _diag_run_phase1(load, lo, hi - lo, low_ref, levs, v_bs, (of_s, ob_s), qg_s, kd_s, ae_s)

    _diag_phase2(v_bs, (of_s, ob_s), qg_s, kd_s, ae_s, n_ctx, n_tot)
    _rms_gate_store(of_s, ob_s, gn, _sigmoid, (gc, gl), (oc, ol), segs)


def _hgrn(p_lat, p_ctx, gamma, gnorm, low, *, batch, layer):
    n_lat = p_lat.shape[0] // batch
    n_ctx = p_ctx.shape[0] // batch
    n_tot = n_lat + n_ctx
    assert n_lat % DROWS == 0 and n_ctx % DROWS == 0

    def blk(n, base):
        return pl.BlockSpec((n, LANES), lambda b, h: (b, base + h))

    names = ("c_q", "c_i", "c_g", "c_f0", "c_f1")
    kern = functools.partial(_hgrn_kernel, n_ctx=n_ctx, n_lat=n_lat, layer=layer)
    return pl.pallas_call(
        kern,
        grid=(batch, GROUP_HEADS),
        in_specs=[blk(n_lat, BLK[k]) for k in names] + [blk(n_ctx, BLK[k]) for k in names]
        + [pl.BlockSpec((2, gamma.shape[1], LANES), lambda b, h: (0, 0, h)),
           pl.BlockSpec((1, LANES), lambda b, h: (0, 0)),
           pl.BlockSpec(low.shape, lambda b, h: (0, 0, 0))],
        out_specs=[pl.BlockSpec((n_lat, LANES), lambda b, h: (b, h)),
                   pl.BlockSpec((n_ctx, LANES), lambda b, h: (b, h))],
        out_shape=[jax.ShapeDtypeStruct((batch * n_lat, GROUP_WIDTH), BF16),
                   jax.ShapeDtypeStruct((batch * n_ctx, GROUP_WIDTH), BF16)],
        scratch_shapes=_diag_scratch(n_tot),
        compiler_params=_cparams(("arbitrary", "arbitrary")),
        name="hgrn",
    )(*([p_lat] * 5), *([p_ctx] * 5), gamma, gnorm.reshape(1, LANES), low)


NA_GROUP = 4
NA_WIN = NA_ROWS + NA_GROUP


def _na_kernel(ws_ref, pat_ref, ql, kl, vl, qc, kc, vc, bias_ref, ol, oc, *, n_groups, win):
    scale = HEAD_DIM ** -0.5
    kcb = kc[...].astype(BF16)
    vcb = vc[...].astype(BF16)
    w = GRID_W
    gq = NA_GROUP * w

    per_step = 2 if n_groups % 2 == 0 else 1

    def groups(it, _):
        gs = [it * per_step + u for u in range(per_step)]
        us = range(per_step)
        qoff = [pl.multiple_of(g * gq, gq) for g in gs]
        koff = [pl.multiple_of(ws_ref[g] * w, w) for g in gs]
        q = [ql[pl.ds(qoff[u], gq), :].astype(BF16) for u in us]
        kb = [kl[pl.ds(koff[u], win * w), :].astype(BF16) for u in us]
        vb = [vl[pl.ds(koff[u], win * w), :].astype(BF16) for u in us]
        s_loc = [_mm_nt(q[u], kb[u]) * scale + bias_ref[0, pat_ref[gs[u]]] for u in us]
        s_ctx = [_mm_nt(q[u], kcb) * scale for u in us]
        m = [jnp.maximum(jnp.max(s_loc[u], axis=-1, keepdims=True), jnp.max(s_ctx[u], axis=-1, keepdims=True))
             for u in us]
        p_loc = [jnp.exp(s_loc[u] - m[u]) for u in us]
        p_ctx = [jnp.exp(s_ctx[u] - m[u]) for u in us]
        den = [jnp.sum(p_loc[u], axis=-1, keepdims=True) + jnp.sum(p_ctx[u], axis=-1, keepdims=True) for u in us]
        o = [_mm(p_loc[u].astype(BF16), vb[u]) + _mm(p_ctx[u].astype(BF16), vcb) for u in us]
        for u in us:
            ol[pl.ds(qoff[u], gq), :] = (o[u] / den[u]).astype(ol.dtype)
        return 0

    lax.fori_loop(0, n_groups // per_step, groups, 0)

    s = _mm_nt(qc[...].astype(BF16), kcb) * scale
    p = jnp.exp(s - jnp.max(s, axis=-1, keepdims=True))
    o = _mm(p.astype(BF16), vcb) / jnp.sum(p, axis=-1, keepdims=True)
    oc[...] = o.astype(oc.dtype)


def _na_plan(n_rows):
    kr = min(NA_ROWS, n_rows)
    win = min(NA_WIN, n_rows)
    n_groups = n_rows // NA_GROUP
    ws = np.clip(np.arange(n_groups) * NA_GROUP - kr // 2, 0, n_rows - win)
    seen, pats, pat_of = {}, [], []
    for g in range(n_groups):
        rows = g * NA_GROUP + np.arange(NA_GROUP)
        rs = np.clip(rows - kr // 2, 0, n_rows - kr)
        krow = ws[g] + np.arange(win)
        ok = (krow[None, :] >= rs[:, None]) & (krow[None, :] < rs[:, None] + kr)
        dr = np.where(ok, krow[None, :] - rows[:, None] + NA_ROWS - 1, 0)
        key = (ok.tobytes(), dr.tobytes())
        if key not in seen:
            seen[key] = len(pats)
            pats.append((ok, dr))
        pat_of.append(seen[key])
    return ws.astype(np.int32), np.asarray(pat_of, np.int32), pats, win


def _na_bias(rpb, pats, win):
    n_h = rpb.shape[0]
    n_p = len(pats)
    ok = np.stack([p[0] for p in pats])
    dr = np.stack([p[1] for p in pats]).reshape(-1)
    row_sel = np.zeros((dr.size, 2 * NA_ROWS - 1), np.float32)
    row_sel[np.arange(dr.size), dr] = 1.0
    cq = np.arange(GRID_W)
    dc = np.clip(cq[None, :] - cq[:, None], -(NA_COLS - 1), NA_COLS - 1) + NA_COLS - 1
    col_start = np.clip(cq - NA_COLS // 2, 0, GRID_W - NA_COLS)
    col_in = (cq[None, :] >= col_start[:, None]) & (cq[None, :] < col_start[:, None] + NA_COLS)
    col_sel = np.zeros((2 * NA_COLS - 1, GRID_W * GRID_W), np.float32)
    col_sel[dc.reshape(-1), np.arange(GRID_W * GRID_W)] = 1.0
    hp = lax.Precision.HIGHEST
    b = jnp.einsum("xr,hrd->hxd", row_sel, rpb.astype(F32), precision=hp)
    b = jnp.einsum("hxd,dq->hxq", b, col_sel, precision=hp)
    b = b.reshape(n_h, n_p, NA_GROUP, win, GRID_W, GRID_W).transpose(0, 1, 2, 4, 3, 5)
    valid = ok[:, :, None, :, None] & col_in[None, None, :, None, :]
    b = jnp.where(valid[None], b, NEG_BIG)
    return b.reshape(n_h, n_p, NA_GROUP * GRID_W, win * GRID_W)


def _na(p_lat, p_ctx, rpb, *, batch):
    n_lat = p_lat.shape[0] // batch
    n_ctx = p_ctx.shape[0] // batch
    n_rows = n_lat // GRID_W
    assert n_rows % NA_GROUP == 0
    ws, pat_of, pats, win = _na_plan(n_rows)
    bias = _na_bias(rpb, pats, win)

    def blk(n, base):
        return pl.BlockSpec((n, LANES), lambda b, h, *_: (b, base + h))

    names = ("d_q", "d_k", "d_v")
    return pl.pallas_call(
        functools.partial(_na_kernel, n_groups=n_rows // NA_GROUP, win=win),
        grid_spec=pltpu.PrefetchScalarGridSpec(
            num_scalar_prefetch=2,
            grid=(batch, GROUP_HEADS),
            in_specs=[blk(n_lat, BLK[k]) for k in names] + [blk(n_ctx, BLK[k]) for k in names]
            + [pl.BlockSpec((1,) + bias.shape[1:], lambda b, h, *_: (h, 0, 0, 0))],
            out_specs=[pl.BlockSpec((n_lat, LANES), lambda b, h, *_: (b, h)),
                       pl.BlockSpec((n_ctx, LANES), lambda b, h, *_: (b, h))]),
        out_shape=[jax.ShapeDtypeStruct((batch * n_lat, GROUP_WIDTH), BF16),
                   jax.ShapeDtypeStruct((batch * n_ctx, GROUP_WIDTH), BF16)],
        compiler_params=_cparams(("arbitrary", "arbitrary")),
        name="na",
    )(jnp.asarray(ws), jnp.asarray(pat_of), *([p_lat] * 3), *([p_ctx] * 3), bias)


def _router_kernel(x_ref, sh_ref, sc_ref, wr_ref, u_ref, code_ref, gate_ref, cum_ref, lg_scr,
                   *, cap, n_tiles, tile):
    j = pl.program_id(1)
    u = _layer_stats(x_ref[...]) * (1.0 + sc_ref[0]) + sh_ref[0]
    u_ref[...] = u.astype(BF16)
    lg_scr[j] = lax.dot_general(wr_ref[...], u, (((1,), (1,)), ((), ())),
                                precision=lax.Precision.HIGHEST, preferred_element_type=F32)

    @pl.when(j == n_tiles - 1)
    def _():
        lg = lg_scr[...]
        ex = jnp.exp(lg - jnp.max(lg, axis=1, keepdims=True))
        aff = ex / jnp.sum(ex, axis=1, keepdims=True)
        bits = lax.bitcast_convert_type(aff, I32)

        def count(mask):
            per = jnp.sum(mask.astype(F32), axis=0)
            return jnp.sum(per, axis=1, keepdims=True)

        def bis(it, thr):
            cand = thr | jnp.left_shift(jnp.int32(1), 30 - it)
            ok = count(bits >= cand[None]) >= float(cap)
            return jnp.where(ok, cand, thr)

        thr = lax.fori_loop(0, 31, bis, jnp.zeros((N_EXPERTS, 1), I32))
        gt = bits > thr[None]
        eq = bits == thr[None]
        need = float(cap) - count(gt)
        r = lax.broadcasted_iota(I32, (tile, tile), 0)
        c = lax.broadcasted_iota(I32, (tile, tile), 1)
        upper = (r < c).astype(BF16)
        carry_eq = jnp.zeros((N_EXPERTS, 1), F32)
        carry_sel = jnp.zeros((N_EXPERTS, 1), F32)
        for t in range(n_tiles):
            eq_t = eq[t].astype(BF16)
            pre_eq = _mm(eq_t, upper) + carry_eq
            sel = gt[t] | (eq[t] & (pre_eq < need))
            sel_b = sel.astype(BF16)
            pos = _mm(sel_b, upper) + carry_sel
            code_ref[0, t] = jnp.where(sel, pos.astype(I32), -1)
            gate_ref[0, t] = aff[t]
            carry_eq = carry_eq + jnp.sum(eq_t.astype(F32), axis=1, keepdims=True)
            carry_sel = carry_sel + jnp.sum(sel_b.astype(F32), axis=1, keepdims=True)
            cum_ref[0, t] = jnp.broadcast_to(carry_sel, (N_EXPERTS, LANES)).astype(I32)


def _router(x, modrows, w_router, *, seg_rows, row0, row_stride):
    n, d = x.shape
    nseg = n // seg_rows
    tile = min(ROUTE_TILE, seg_rows)
    n_tiles = seg_rows // tile
    cap = EC_CAPACITY * seg_rows // N_EXPERTS
    kern = functools.partial(_router_kernel, cap=cap, n_tiles=n_tiles, tile=tile)
    return pl.pallas_call(
        kern,
        grid=(nseg, n_tiles),
        in_specs=[pl.BlockSpec((tile, d), lambda s, j: (s * n_tiles + j, 0)),
                  pl.BlockSpec((1, 1, d), lambda s, j: ((row0 + s * row_stride) * 6 + 3, 0, 0)),
                  pl.BlockSpec((1, 1, d), lambda s, j: ((row0 + s * row_stride) * 6 + 4, 0, 0)),
                  pl.BlockSpec((N_EXPERTS, d), lambda s, j: (0, 0))],
        out_specs=[pl.BlockSpec((tile, d), lambda s, j: (s * n_tiles + j, 0)),
                   pl.BlockSpec((1, n_tiles, N_EXPERTS, tile), lambda s, j: (s, 0, 0, 0)),
                   pl.BlockSpec((1, n_tiles, N_EXPERTS, tile), lambda s, j: (s, 0, 0, 0)),
                   pl.BlockSpec((1, n_tiles, N_EXPERTS, LANES), lambda s, j: (s, 0, 0, 0))],
        out_shape=[jax.ShapeDtypeStruct((n, d), BF16),
                   jax.ShapeDtypeStruct((nseg, n_tiles, N_EXPERTS, tile), I32),
                   jax.ShapeDtypeStruct((nseg, n_tiles, N_EXPERTS, tile), F32),
                   jax.ShapeDtypeStruct((nseg, n_tiles, N_EXPERTS, LANES), I32)],
        scratch_shapes=[pltpu.VMEM((n_tiles, N_EXPERTS, tile), F32)],
        compiler_params=_cparams(("arbitrary", "arbitrary")),
        name="router",
    )(x, modrows, modrows, w_router.T)


SLOT_TILE = 128


def _slot_tiles_touched(cum_ref, base, t, ts, n_slot_tiles):
    lo = cum_ref[base + t]
    hi = cum_ref[base + t + 1]
    return [(lo < (i + 1) * ts) & (hi > i * ts) for i in range(n_slot_tiles)]


def _gather_kernel(cum_ref, u_ref, code_ref, o_ref, acc, *, cap, n_tiles, tile, ts):
    base = (pl.program_id(0) * N_EXPERTS + pl.program_id(1)) * (n_tiles + 1)
    acc[...] = jnp.zeros_like(acc)
    slot = lax.broadcasted_iota(I32, (ts, tile), 0)
    for t in range(n_tiles):
        for i, touched in enumerate(_slot_tiles_touched(cum_ref, base, t, ts, cap // ts)):
            @pl.when(touched)
            def _(t=t, i=i):
                oh = (code_ref[0, 0, t:t + 1, :] == slot + i * ts).astype(BF16)
                acc[i * ts:(i + 1) * ts, :] += _mm(oh, u_ref[t * tile:(t + 1) * tile, :])
    o_ref[0, 0] = acc[...].astype(BF16)


def _gather(u, code_e, cum_flat, *, seg_rows):
    n, d = u.shape
    nseg, _, n_tiles, tile = code_e.shape
    cap = EC_CAPACITY * seg_rows // N_EXPERTS
    ts = min(SLOT_TILE, cap)
    return pl.pallas_call(
        functools.partial(_gather_kernel, cap=cap, n_tiles=n_tiles, tile=tile, ts=ts),
        grid_spec=pltpu.PrefetchScalarGridSpec(
            num_scalar_prefetch=1,
            grid=(nseg, N_EXPERTS),
            in_specs=[pl.BlockSpec((seg_rows, d), lambda s, e, *_: (s, 0)),
                      pl.BlockSpec((1, 1, n_tiles, tile), lambda s, e, *_: (s, e, 0, 0))],
            out_specs=pl.BlockSpec((1, 1, cap, d), lambda s, e, *_: (e, s, 0, 0)),
            scratch_shapes=[pltpu.VMEM((cap, d), F32)]),
        out_shape=jax.ShapeDtypeStruct((N_EXPERTS, nseg, cap, d), BF16),
        compiler_params=_cparams(("arbitrary", "arbitrary")),
        name="moe_gather",
    )(cum_flat, u, code_e)


def _ffn_kernel(x_ref, w1_ref, w3_ref, w2_ref, o_ref):
    x = x_ref[0]
    a = _mm(x, w1_ref[0, 0])
    g = _mm(x, w3_ref[0, 0])
    hid = (_silu(a) * g).astype(BF16)
    o_ref[0] = _mm(hid, w2_ref[0, 0]).astype(BF16)


def _ffn(xs, w1, w3, w2, layer):
    e, r, d = xs.shape
    f = w1.shape[3]
    tr = min(512, r)
    return pl.pallas_call(
        _ffn_kernel,
        grid=(e, r // tr),
        in_specs=[pl.BlockSpec((1, tr, d), lambda i, j: (i, j, 0)),
                  pl.BlockSpec((1, 1, d, f), lambda i, j: (layer, i, 0, 0)),
                  pl.BlockSpec((1, 1, d, f), lambda i, j: (layer, i, 0, 0)),
                  pl.BlockSpec((1, 1, f, d), lambda i, j: (layer, i, 0, 0))],
        out_specs=pl.BlockSpec((1, tr, d), lambda i, j: (i, j, 0)),
        out_shape=jax.ShapeDtypeStruct((e, r, d), BF16),
        compiler_params=_cparams(("arbitrary", "arbitrary")),
        name="moe_ffn",
    )(xs, w1, w3, w2)


COMBINE_EXPERTS = 4
COMBINE_SMALL_ROWS = 48
BF16_ROWS = 16


def _window_start(cum_ref, s, e, j, n_tiles, cap, wr):
    lo = cum_ref[(s * N_EXPERTS + e) * (n_tiles + 1) + j]
    return jnp.minimum((lo // BF16_ROWS) * BF16_ROWS, cap - wr)


def _needs_big(cum_ref, s, e, j, n_tiles, cap, ws):
    hi = cum_ref[(s * N_EXPERTS + e) * (n_tiles + 1) + j + 1]
    return hi > _window_start(cum_ref, s, e, j, n_tiles, cap, ws) + ws


def _combine_kernel(cum_ref, *refs, cap, n_tiles, ws, wr):
    two = wr > ws
    small_refs = refs[:COMBINE_EXPERTS]
    big_refs = refs[COMBINE_EXPERTS:2 * COMBINE_EXPERTS] if two else small_refs
    code_ref, gate_ref, x_ref, m5_ref, g_ref, b_ref, o_ref, acc = refs[(2 if two else 1) * COMBINE_EXPERTS:]
    s = pl.program_id(0)
    j = pl.program_id(1)
    eg = pl.program_id(2)

    @pl.when(eg == 0)
    def _():
        acc[...] = jnp.zeros_like(acc)

    tj = code_ref.shape[1]
    lane = lax.broadcasted_iota(I32, (tj, N_EXPERTS), 1)

    def add_rows(e, y_ref, start, rows):
        code = jnp.sum(jnp.where(lane == e, code_ref[0], 0), axis=1, keepdims=True)
        gate = jnp.sum(jnp.where(lane == e, gate_ref[0], 0.0), axis=1, keepdims=True)
        main = min(rows, ROUTE_TILE)
        slot = start + lax.broadcasted_iota(I32, (tj, main), 1)
        acc[...] += _mm(jnp.where(code == slot, gate, 0.0).astype(BF16), y_ref[0:main, :])
        if rows > main:
            slot2 = start + main + lax.broadcasted_iota(I32, (tj, rows - main), 1)
            acc[...] += _mm(jnp.where(code == slot2, gate, 0.0).astype(BF16), y_ref[main:rows, :])

    col = lax.broadcasted_iota(I32, (tj, COMBINE_EXPERTS * ws), 1)
    wm = jnp.zeros((tj, COMBINE_EXPERTS * ws), F32)
    bigs = []
    for k in range(COMBINE_EXPERTS):
        e = eg * COMBINE_EXPERTS + k
        code = jnp.sum(jnp.where(lane == e, code_ref[0], 0), axis=1, keepdims=True)
        gate = jnp.sum(jnp.where(lane == e, gate_ref[0], 0.0), axis=1, keepdims=True)
        big = _needs_big(cum_ref, s, e, j, n_tiles, cap, ws) if two else None
        bigs.append(big)
        if two:
            gate = jnp.where(big, 0.0, gate)
        slot = _window_start(cum_ref, s, e, j, n_tiles, cap, ws) + (col - k * ws)
        wm = jnp.where((col >= k * ws) & (col < (k + 1) * ws) & (code == slot), gate, wm)
    ycat = jnp.concatenate([small_refs[k][...] for k in range(COMBINE_EXPERTS)], axis=0)
    acc[...] += _mm(wm.astype(BF16), ycat)

    if two:
        for k in range(COMBINE_EXPERTS):
            @pl.when(bigs[k])
            def _(k=k):
                e = eg * COMBINE_EXPERTS + k
                add_rows(e, big_refs[k], _window_start(cum_ref, s, e, j, n_tiles, cap, wr), wr)

    @pl.when(eg == N_EXPERTS // COMBINE_EXPERTS - 1)
    def _():
        y = DEEPNORM_ALPHA * x_ref[...] + m5_ref[0] * acc[...]
        o_ref[...] = _layer_stats(y) * g_ref[...] + b_ref[...]


def _combine(ys, code_t, gate_t, cum_flat, x, modrows, g, b, *, seg_rows, tile, row0, row_stride):
    n, d = x.shape
    nseg = n // seg_rows
    cap = ys.shape[2]
    tj = tile
    nt = seg_rows // tj
    wr = min(tj + BF16_ROWS, cap)
    ws = min(COMBINE_SMALL_ROWS + BF16_ROWS, cap)
    two = wr > ws

    def window(k, rows, only_if_big):
        def index(s, j, eg, cum_ref):
            e = eg * COMBINE_EXPERTS + k
            row = (e * nseg + s) * cap + _window_start(cum_ref, s, e, j, nt, cap, rows)
            if only_if_big:
                row = jnp.where(_needs_big(cum_ref, s, e, j, nt, cap, ws), row, 0)
            return pl.multiple_of(row, BF16_ROWS), 0
        return pl.BlockSpec((pl.Element(rows), pl.Element(d)), index)

    windows = [window(k, ws, False) for k in range(COMBINE_EXPERTS)]
    if two:
        windows += [window(k, wr, True) for k in range(COMBINE_EXPERTS)]
    y2d = ys.reshape(N_EXPERTS * nseg * cap, d)
    return pl.pallas_call(
        functools.partial(_combine_kernel, cap=cap, n_tiles=nt, ws=ws, wr=wr),
        grid_spec=pltpu.PrefetchScalarGridSpec(
            num_scalar_prefetch=1,
            grid=(nseg, nt, N_EXPERTS // COMBINE_EXPERTS),
            in_specs=windows
            + [pl.BlockSpec((1, tj, N_EXPERTS), lambda s, j, e, *_: (s, j, 0)),
               pl.BlockSpec((1, tj, N_EXPERTS), lambda s, j, e, *_: (s, j, 0)),
               pl.BlockSpec((tj, d), lambda s, j, e, *_: (s * nt + j, 0)),
               pl.BlockSpec((1, 1, d), lambda s, j, e, *_: ((row0 + s * row_stride) * 6 + 5, 0, 0)),
               pl.BlockSpec((1, d), lambda s, j, e, *_: (0, 0)),
               pl.BlockSpec((1, d), lambda s, j, e, *_: (0, 0))],
            out_specs=pl.BlockSpec((tj, d), lambda s, j, e, *_: (s * nt + j, 0)),
            scratch_shapes=[pltpu.VMEM((tj, d), F32)]),
        out_shape=jax.ShapeDtypeStruct((n, d), F32),
        compiler_params=_cparams(("arbitrary", "arbitrary", "arbitrary")),
        name="moe_combine",
    )(cum_flat, *([y2d] * len(windows)), code_t, gate_t, x, modrows, g.reshape(1, d), b.reshape(1, d))


def _moe(x, modrows, w_router, w1, w3, w2, layer, g, b, *, seg_rows, row0, row_stride):
    n, d = x.shape
    nseg = n // seg_rows
    u, code, gate, cum = _router(x, modrows, w_router, seg_rows=seg_rows, row0=row0, row_stride=row_stride)
    tile = code.shape[3]
    code_e = code.transpose(0, 2, 1, 3)
    code_t = code.transpose(0, 1, 3, 2).reshape(nseg, seg_rows, N_EXPERTS)
    gate_t = gate.transpose(0, 1, 3, 2).reshape(nseg, seg_rows, N_EXPERTS)
    cum_e = cum[..., 0].transpose(0, 2, 1)
    cum_x = jnp.pad(cum_e, ((0, 0), (0, 0), (1, 0)))
    cum_flat = cum_x.reshape(-1)
    n_tiles = code.shape[1]
    gm = 2 if n_tiles % 2 == 0 else 1
    xs = _gather(u, code_e.reshape(nseg, N_EXPERTS, n_tiles // gm, tile * gm), cum_x[:, :, ::gm].reshape(-1),
                 seg_rows=seg_rows)
    cap = xs.shape[2]
    ys = _ffn(xs.reshape(N_EXPERTS, nseg * cap, d), w1, w3, w2, layer).reshape(N_EXPERTS, nseg, cap, d)
    return _combine(ys, code_t, gate_t, cum_flat, x, modrows, g, b, seg_rows=seg_rows, tile=tile,
                    row0=row0, row_stride=row_stride)


def kernel(x, c, ctx, c_ctx, w_mod, b_mod, w_in, w_out, ln_g, ln_b, gdn_conv, gdn_a_log, gdn_dt_bias,
           gdn_norm, gla_w_gate, gla_b_gate, gla_norm, hgrn_gamma, hgrn_norm, na_rpb,
           moe_router, moe_w1, moe_w3, moe_w2):
    batch, seq, d = x.shape
    n_ctx = ctx.shape[1]
    xl = x.reshape(batch * seq, d)
    xc = ctx.reshape(batch * n_ctx, d)

    cc = jnp.zeros((8, d), F32).at[:batch].set(c).at[batch].set(c_ctx)
    mods = _modulation(cc, w_mod, b_mod)

    w_in_r = _regroup_w_in(w_in)
    w_out_b = w_out.astype(BF16)
    w1_b, w3_b, w2_b = moe_w1.astype(BF16), moe_w3.astype(BF16), moe_w2.astype(BF16)
    low = jnp.asarray(_LOW_MATS, BF16)

    for l in range(DEPTH):
        keep_ctx = l < DEPTH - 1
        modrows = mods[l].reshape(8 * 6, 1, d)
        p_lat = _inproj(xl, modrows, w_in_r, l, seg_rows=seq, row0=0)
        p_ctx = _inproj(xc, modrows, w_in_r, l, seg_rows=batch * n_ctx, row0=batch)
        mixes = [
            _gdn(p_lat, p_ctx, gdn_conv[l], gdn_a_log[l], gdn_dt_bias[l], gdn_norm[l], batch=batch),
            _gla(p_lat, p_ctx, gla_w_gate[l], gla_b_gate[l], gla_norm[l], low, batch=batch),
            _hgrn(p_lat, p_ctx, hgrn_gamma, hgrn_norm[l], low, batch=batch, layer=l),
            _na(p_lat, p_ctx, na_rpb[l], batch=batch),
        ]
        xl = _outproj([m[0] for m in mixes], w_out_b, l, xl, modrows, ln_g[l, 0], ln_b[l, 0],
                      seg_rows=seq, row0=0)
        if keep_ctx:
            xc = _outproj([m[1] for m in mixes], w_out_b, l, xc, modrows, ln_g[l, 0], ln_b[l, 0],
                          seg_rows=batch * n_ctx, row0=batch)
        moe_args = (moe_router[l], w1_b, w3_b, w2_b, l, ln_g[l, 1], ln_b[l, 1])
        xl = _moe(xl, modrows, *moe_args, seg_rows=seq, row0=0, row_stride=1)
        if keep_ctx:
            xc = _moe(xc, modrows, *moe_args, seg_rows=n_ctx, row0=batch, row_stride=0)
    return xl.reshape(batch, seq, d)
```
